```python
import math
import jax, jax.numpy as jnp
from jax import lax
import numpy as np

D_MODEL = 1024
BATCH = 4
SEQ = 4096
DEPTH = 2

GRID_W = 64
CTX_LEN = 256

ATTN_HEADS = 8
ATTN_KV_HEADS = 2
HEAD_DIM = 64
ATTN_REP = ATTN_HEADS // ATTN_KV_HEADS
ATTN_WIDTH = ATTN_HEADS * HEAD_DIM
KV_WIDTH = ATTN_KV_HEADS * HEAD_DIM
AXIS_ROT = HEAD_DIM // 2
ROPE_THETA = 10000.0
Q_BLOCK = 128

SSD_HEADS = 8
SSD_HEAD_DIM = 64
SSD_INNER = SSD_HEADS * SSD_HEAD_DIM
SSD_STATE = 64
SSD_GROUPS = 2
SSD_HPG = SSD_HEADS // SSD_GROUPS
SSD_GN = SSD_GROUPS * SSD_STATE
SSD_CONV = 3
SSD_CONV_DIM = SSD_INNER + 2 * SSD_GN
SSD_CHUNK = 128

SC_WIDTH = 512
SC_CONV = 3

N_BRANCH = 3

N_EXPERTS = 16
EXPERT_FF = 2048
CAPACITY_FACTOR = 2

ALPHA = (2 * DEPTH) ** 0.25
BETA = (8 * DEPTH) ** -0.25

N_MOD = 6
EPS = 1e-6

CTX_COLS = (KV_WIDTH, KV_WIDTH, SSD_CONV_DIM, 2 * SSD_HEADS)
LAT_COLS = (ATTN_WIDTH, SSD_INNER, SC_WIDTH, SC_WIDTH, SC_WIDTH, N_BRANCH * D_MODEL)
CTX_STATE_COLS = sum(CTX_COLS)
IN_COLS = CTX_STATE_COLS + sum(LAT_COLS)

kernel_name = "hybrid_dit_attn_ssd_shortconv_ecmoe"


def split_cols(p, sizes):
    return jnp.split(p, np.cumsum(sizes)[:-1].tolist(), axis=-1)


def layer_norm(x, g=None, b=None):
    xf = x.astype(jnp.float32)
    mu = jnp.mean(xf, axis=-1, keepdims=True)
    var = jnp.mean(jnp.square(xf - mu), axis=-1, keepdims=True)
    y = (xf - mu) * lax.rsqrt(var + EPS)
    if g is not None:
        y = y * g.astype(jnp.float32) + b.astype(jnp.float32)
    return y.astype(x.dtype)


def rms_norm(x, g):
    xf = x.astype(jnp.float32)
    y = xf * lax.rsqrt(jnp.mean(jnp.square(xf), axis=-1, keepdims=True) + EPS)
    return (y * g.astype(jnp.float32)).astype(x.dtype)


def modulate(x, shift, scale):
    return layer_norm(x) * (1.0 + scale) + shift


def dwconv_centred(u, w):
    k = w.shape[0]
    return lax.conv_general_dilated(
        u, w[:, None, :].astype(u.dtype), window_strides=(1,),
        padding=[(k // 2, k // 2)], dimension_numbers=('NWC', 'WIO', 'NWC'),
        feature_group_count=u.shape[-1])


def axial_rope(n):
    rows = n // GRID_W
    row = jnp.repeat(jnp.arange(rows), GRID_W).astype(jnp.float32)
    col = jnp.tile(jnp.arange(GRID_W), rows).astype(jnp.float32)
    inv = ROPE_THETA ** (-jnp.arange(0, AXIS_ROT, 2, dtype=jnp.float32) / AXIS_ROT)
    ang = jnp.stack([row[:, None] * inv, col[:, None] * inv], axis=1)
    return jnp.cos(ang), jnp.sin(ang)


def apply_rope(x, cos, sin):
    b, n, h, d = x.shape
    xa = x.reshape(b, n, h, 2, 2, AXIS_ROT // 2).astype(jnp.float32)
    x1, x2 = xa[..., 0, :], xa[..., 1, :]
    c, s = cos[None, :, None], sin[None, :, None]
    out = jnp.stack([x1 * c - x2 * s, x2 * c + x1 * s], axis=-2)
    return out.reshape(b, n, h, d).astype(x.dtype)


def attend(q, k, v):
    b, tq = q.shape[:2]
    qg = q.reshape(b, tq, ATTN_KV_HEADS, ATTN_REP, HEAD_DIM)
    s = jnp.einsum('bqgrd,bkgd->bgrqk', qg, k, preferred_element_type=jnp.float32) * (HEAD_DIM ** -0.5)
    p = jax.nn.softmax(s, axis=-1).astype(v.dtype)
    o = jnp.einsum('bgrqk,bkgd->bqgrd', p, v)
    return o.reshape(b, tq, ATTN_WIDTH)


def blocked_attention(q, k, v):
    b, n = q.shape[:2]
    nb = n // Q_BLOCK
    qb = q.reshape(b, nb, Q_BLOCK, ATTN_HEADS, HEAD_DIM).swapaxes(0, 1)
    ob = lax.map(lambda blk: attend(blk, k, v), qb)
    return ob.swapaxes(0, 1).reshape(b, n, ATTN_WIDTH)


def ssd_scan(xs, dt, a, bm, cm, h0):
    b, t, nh, hp = xs.shape
    L = SSD_CHUNK
    nc = t // L
    to_chunks = lambda u: u.reshape((b, nc, L) + u.shape[2:]).astype(jnp.float32)
    xc, dtc, bc, cc = to_chunks(xs), to_chunks(dt), to_chunks(bm), to_chunks(cm)
    acs = jnp.cumsum(dtc * a, axis=2)
    mask = jnp.tril(jnp.ones((L, L), dtype=bool))
    seg = acs[:, :, :, None, :] - acs[:, :, None, :, :]
    decay = jnp.exp(jnp.where(mask[:, :, None], seg, -jnp.inf))
    scores = jnp.einsum('bclhn,bcshn->bclsh', cc, bc) * decay
    y_diag = jnp.einsum('bclsh,bcsh,bcshp->bclhp', scores, dtc, xc)
    w_state = jnp.exp(acs[:, :, -1:, :] - acs) * dtc
    states = jnp.einsum('bclhn,bclh,bclhp->bchpn', bc, w_state, xc)
    chunk_decay = jnp.exp(acs[:, :, -1, :])

    def step(h, inp):
        st, dec = inp
        return dec[:, :, None, None] * h + st, h

    h_last, h_start = lax.scan(step, h0.astype(jnp.float32),
                               (jnp.moveaxis(states, 1, 0), jnp.moveaxis(chunk_decay, 1, 0)))
    h_start = jnp.moveaxis(h_start, 0, 1)
    y_off = jnp.einsum('bclhn,bchpn,bclh->bclhp', cc, h_start, jnp.exp(acs))
    return (y_diag + y_off).reshape(b, t, nh, hp), h_last


def ssd_bidirectional(xbc_raw, dt_raw, conv_w, conv_b, a_log, dt_bias, d_skip, h0_f, h0_b):
    b, t, _ = xbc_raw.shape
    xbc = jax.nn.silu(dwconv_centred(xbc_raw, conv_w) + conv_b)
    xs, bm, cm = split_cols(xbc, (SSD_INNER, SSD_GN, SSD_GN))
    xs = xs.reshape(b, t, SSD_HEADS, SSD_HEAD_DIM)
    bm = jnp.repeat(bm.reshape(b, t, SSD_GROUPS, SSD_STATE), SSD_HPG, axis=2)
    cm = jnp.repeat(cm.reshape(b, t, SSD_GROUPS, SSD_STATE), SSD_HPG, axis=2)
    dt = jax.nn.softplus(dt_raw.reshape(b, t, 2, SSD_HEADS).astype(jnp.float32)
                         + dt_bias.astype(jnp.float32))
    a = -jnp.exp(a_log.astype(jnp.float32))
    y_f, s_f = ssd_scan(xs, dt[:, :, 0], a[0], bm, cm, h0_f)
    flip = lambda u: jnp.flip(u, axis=1)
    y_b, s_b = ssd_scan(flip(xs), flip(dt[:, :, 1]), a[1], flip(bm), flip(cm), h0_b)
    y = y_f + flip(y_b) + d_skip.astype(jnp.float32)[:, None] * xs.astype(jnp.float32)
    return y.reshape(b, t, SSD_INNER).astype(xbc_raw.dtype), s_f, s_b


def merge_branches(ya, ys, yc, gate_cols, w_br_attn, w_br_ssd, w_br_conv, w_o):
    b, t = ya.shape[:2]
    g = jax.nn.sigmoid(gate_cols).reshape(b, t, N_BRANCH, D_MODEL)
    m = (g[:, :, 0] * (ya @ w_br_attn) + g[:, :, 1] * (ys @ w_br_ssd)
         + g[:, :, 2] * (yc @ w_br_conv))
    return m @ w_o


def token_mixer(h_ctx, h_lat, cos, sin, w_in, q_norm, k_norm, ssd_conv_w, ssd_conv_b,
                ssd_a_log, ssd_dt_bias, ssd_d, ssd_norm, sc_conv_w,
                w_br_attn, w_br_ssd, w_br_conv, w_o, need_ctx):
    b, n = h_lat.shape[:2]
    nctx = h_ctx.shape[1]
    if need_ctx:
        pc = split_cols(h_ctx @ w_in, CTX_COLS + LAT_COLS)
    else:
        pc = split_cols(h_ctx @ w_in[:, :CTX_STATE_COLS], CTX_COLS)
    k_c = rms_norm(pc[0].reshape(b, nctx, ATTN_KV_HEADS, HEAD_DIM), k_norm)
    v_c = pc[1].reshape(b, nctx, ATTN_KV_HEADS, HEAD_DIM)
    zeros_state = jnp.zeros((b, SSD_HEADS, SSD_HEAD_DIM, SSD_STATE), jnp.float32)
    ys_c, s_f, s_b = ssd_bidirectional(pc[2], pc[3], ssd_conv_w, ssd_conv_b, ssd_a_log,
                                       ssd_dt_bias, ssd_d, zeros_state, zeros_state)

    k_l, v_l, xbc_l, dt_l, q_l, z_l, scb_l, scc_l, scx_l, gate_l = split_cols(h_lat @ w_in, CTX_COLS + LAT_COLS)
    q_l = apply_rope(rms_norm(q_l.reshape(b, n, ATTN_HEADS, HEAD_DIM), q_norm), cos, sin)
    k_l = apply_rope(rms_norm(k_l.reshape(b, n, ATTN_KV_HEADS, HEAD_DIM), k_norm), cos, sin)
    keys = jnp.concatenate([k_c, k_l], axis=1)
    vals = jnp.concatenate([v_c, v_l.reshape(b, n, ATTN_KV_HEADS, HEAD_DIM)], axis=1)
    ya_l = blocked_attention(q_l, keys, vals)
    ys_l, _, _ = ssd_bidirectional(xbc_l, dt_l, ssd_conv_w, ssd_conv_b, ssd_a_log,
                                   ssd_dt_bias, ssd_d, s_f, s_b)
    ys_l = rms_norm(ys_l * jax.nn.silu(z_l), ssd_norm)
    yc_l = scb_l * dwconv_centred(scc_l * scx_l, sc_conv_w)
    out_l = merge_branches(ya_l, ys_l, yc_l, gate_l, w_br_attn, w_br_ssd, w_br_conv, w_o)

    out_c = None
    if need_ctx:
        q_c, z_c, scb_c, scc_c, scx_c, gate_c = pc[4:]
        q_c = rms_norm(q_c.reshape(b, nctx, ATTN_HEADS, HEAD_DIM), q_norm)
        ya_c = attend(q_c, k_c, v_c)
        ys_cg = rms_norm(ys_c * jax.nn.silu(z_c), ssd_norm)
        yc_c = scb_c * dwconv_centred(scc_c * scx_c, sc_conv_w)
        out_c = merge_branches(ya_c, ys_cg, yc_c, gate_c, w_br_attn, w_br_ssd, w_br_conv, w_o)
    return out_l, out_c


def ec_moe(h, w_router, w_gate, w_up, w_down):
    b, t, _ = h.shape
    cap = CAPACITY_FACTOR * t // N_EXPERTS
    aff = jax.nn.softmax((h @ w_router).astype(jnp.float32), axis=-1)
    top_aff, idx = lax.top_k(aff.swapaxes(1, 2), cap)
    bidx = jnp.arange(b)[:, None, None]
    xs = h[bidx, idx]
    hid = jax.nn.silu(jnp.einsum('becd,edf->becf', xs, w_gate)) * jnp.einsum('becd,edf->becf', xs, w_up)
    out = jnp.einsum('becf,efd->becd', hid, w_down) * top_aff[..., None].astype(h.dtype)
    return jnp.zeros_like(h).at[bidx, idx].add(out)


def setup_inputs(seed: int = 0) -> dict:
    key = jax.random.key(seed)
    ks = jax.random.split(key, 28)
    f32 = jnp.float32
    nrm = lambda k, shape, scale: scale * jax.random.normal(k, shape, f32)
    L, D = DEPTH, D_MODEL
    dt0 = jnp.exp(jax.random.uniform(ks[12], (L, 2, SSD_HEADS), f32,
                                     minval=math.log(1e-3), maxval=math.log(1e-1)))
    return {
        "x": nrm(ks[0], (BATCH, SEQ, D), 1.0),
        "c": nrm(ks[1], (BATCH, D), 1.0),
        "ctx": nrm(ks[2], (BATCH, CTX_LEN, D), 1.0),
        "c_ctx": nrm(ks[3], (D,), 1.0),
        "w_mod": nrm(ks[4], (L, D, N_MOD * D), 0.5 * D ** -0.5),
        "b_mod": nrm(ks[5], (L, N_MOD * D), 0.01),
        "w_in": nrm(ks[6], (L, D, IN_COLS), D ** -0.5),
        "q_norm": 1.0 + nrm(ks[7], (L, HEAD_DIM), 0.05),
        "k_norm": 1.0 + nrm(ks[8], (L, HEAD_DIM), 0.05),
        "ssd_conv_w": nrm(ks[9], (L, SSD_CONV, SSD_CONV_DIM), SSD_CONV ** -0.5),
        "ssd_conv_b": nrm(ks[10], (L, SSD_CONV_DIM), 0.01),
        "ssd_a_log": jnp.log(jax.random.uniform(ks[11], (L, 2, SSD_HEADS), f32, minval=1.0, maxval=16.0)),
        "ssd_dt_bias": dt0 + jnp.log(-jnp.expm1(-dt0)),
        "ssd_d": 1.0 + nrm(ks[13], (L, SSD_HEADS), 0.1),
        "ssd_norm": 1.0 + nrm(ks[14], (L, SSD_INNER), 0.05),
        "sc_conv_w": nrm(ks[15], (L, SC_CONV, SC_WIDTH), SC_CONV ** -0.5),
        "w_br_attn": nrm(ks[16], (L, ATTN_WIDTH, D), ATTN_WIDTH ** -0.5),
        "w_br_ssd": nrm(ks[17], (L, SSD_INNER, D), SSD_INNER ** -0.5),
        "w_br_conv": nrm(ks[18], (L, SC_WIDTH, D), SC_WIDTH ** -0.5),
        "w_o": nrm(ks[19], (L, D, D), BETA * D ** -0.5),
        "ln1_g": 1.0 + nrm(ks[20], (L, D), 0.05),
        "ln1_b": nrm(ks[21], (L, D), 0.01),
        "w_router": nrm(ks[22], (L, D, N_EXPERTS), D ** -0.5),
        "w_exp_gate": nrm(ks[23], (L, N_EXPERTS, D, EXPERT_FF), D ** -0.5),
        "w_exp_up": nrm(ks[24], (L, N_EXPERTS, D, EXPERT_FF), D ** -0.5),
        "w_exp_down": nrm(ks[25], (L, N_EXPERTS, EXPERT_FF, D), BETA * EXPERT_FF ** -0.5),
        "ln2_g": 1.0 + nrm(ks[26], (L, D), 0.05),
        "ln2_b": nrm(ks[27], (L, D), 0.01),
    }


def reference(x, c, ctx, c_ctx, w_mod, b_mod, w_in, q_norm, k_norm, ssd_conv_w, ssd_conv_b,
              ssd_a_log, ssd_dt_bias, ssd_d, ssd_norm, sc_conv_w, w_br_attn, w_br_ssd,
              w_br_conv, w_o, ln1_g, ln1_b, w_router, w_exp_gate, w_exp_up, w_exp_down,
              ln2_g, ln2_b):
    n = x.shape[1]
    cos, sin = axial_rope(n)
    x_ctx = ctx
    silu_c = jax.nn.silu(c)
    silu_cc = jax.nn.silu(c_ctx)[None]
    for i in range(DEPTH):
        need_ctx = i < DEPTH - 1
        mod_l = [m[:, None, :] for m in jnp.split(silu_c @ w_mod[i] + b_mod[i], N_MOD, axis=-1)]
        mod_c = [m[:, None, :] for m in jnp.split(silu_cc @ w_mod[i] + b_mod[i], N_MOD, axis=-1)]

        h_l = modulate(x, mod_l[0], mod_l[1])
        h_c = modulate(x_ctx, mod_c[0], mod_c[1])
        out_l, out_c = token_mixer(h_c, h_l, cos, sin, w_in[i], q_norm[i], k_norm[i],
                                   ssd_conv_w[i], ssd_conv_b[i], ssd_a_log[i], ssd_dt_bias[i],
                                   ssd_d[i], ssd_norm[i], sc_conv_w[i], w_br_attn[i],
                                   w_br_ssd[i], w_br_conv[i], w_o[i], need_ctx)
        x = layer_norm(ALPHA * x + mod_l[2] * out_l, ln1_g[i], ln1_b[i])

        moe_l = ec_moe(modulate(x, mod_l[3], mod_l[4]), w_router[i], w_exp_gate[i],
                       w_exp_up[i], w_exp_down[i])
        x = layer_norm(ALPHA * x + mod_l[5] * moe_l, ln2_g[i], ln2_b[i])

        if need_ctx:
            x_ctx = layer_norm(ALPHA * x_ctx + mod_c[2] * out_c, ln1_g[i], ln1_b[i])
            moe_c = ec_moe(modulate(x_ctx, mod_c[3], mod_c[4]), w_router[i], w_exp_gate[i],
                           w_exp_up[i], w_exp_down[i])
            x_ctx = layer_norm(ALPHA * x_ctx + mod_c[5] * moe_c, ln2_g[i], ln2_b[i])
    return x
```

```python
import functools
import math

import jax
import jax.numpy as jnp
from jax import lax
from jax.experimental import pallas as pl
from jax.experimental.pallas import tpu as pltpu

F32, BF16, I32 = jnp.float32, jnp.bfloat16, jnp.int32

HEAD_DIM = 64
ATTN_HEADS = 8
KV_HEADS = 2
ATTN_REP = ATTN_HEADS // KV_HEADS
ATTN_WIDTH = ATTN_HEADS * HEAD_DIM
KV_WIDTH = KV_HEADS * HEAD_DIM
AXIS_ROT = HEAD_DIM // 2
ROPE_THETA = 10000.0
GRID_W = 64
SSD_HEADS = 8
SSD_HEAD_DIM = 64
SSD_INNER = SSD_HEADS * SSD_HEAD_DIM
SSD_STATE = 64
SSD_GROUPS = 2
SSD_GN = SSD_GROUPS * SSD_STATE
SSD_CONV_DIM = SSD_INNER + 2 * SSD_GN
SSD_CHUNK = 128
SC_WIDTH = 512
N_BRANCH = 3
N_EXPERTS = 16
CAPACITY_FACTOR = 2
N_MOD = 6
EPS = 1e-6

LANES = 128
BF16_ROWS = 16
VMEM_LIMIT_BYTES = 56 * 1024 * 1024

COL_GATE, COL_XBC, COL_KV, COL_Q, COL_Z, COL_SCB, COL_SCC, COL_SCX = 0, 3072, 3840, 4096, 4608, 5120, 5632, 6144
MAIN_COLS = 6656

_NT = (((1,), (1,)), ((), ()))
_TN = (((0,), (0,)), ((), ()))


def _cparams(*sem):
    return pltpu.CompilerParams(dimension_semantics=sem, vmem_limit_bytes=VMEM_LIMIT_BYTES)


def _dot(a, b):
    return jnp.dot(a, b, preferred_element_type=F32)


def _split2(a):
    hi = a.astype(BF16)
    lo = (a - hi.astype(F32)).astype(BF16)
    return hi, lo


def _split3(a):
    p1 = a.astype(BF16)
    r1 = a - p1.astype(F32)
    p2 = r1.astype(BF16)
    p3 = (r1 - p2.astype(F32)).astype(BF16)
    return p1, p2, p3


def _dot_hp(a, w):
    ah, al = _split2(a)
    wh, wl = _split2(w)
    return _dot(ah, wh) + _dot(al, wh) + _dot(ah, wl)


def _dot_hp_nt(a, w):
    ah, al = _split2(a)
    wh, wl = _split2(w)
    d = lambda x, y: lax.dot_general(x, y, _NT, preferred_element_type=F32)
    return d(ah, wh) + d(al, wh) + d(ah, wl)


def _dot_exact_lhs(a, m):
    p1, p2, p3 = _split3(a)
    return _dot(p1, m) + _dot(p2, m) + _dot(p3, m)


def _layer_norm(x):
    mu = jnp.mean(x, axis=-1, keepdims=True)
    xc = x - mu
    var = jnp.mean(xc * xc, axis=-1, keepdims=True)
    return xc * lax.rsqrt(var + EPS)


def _softplus(x):
    return jnp.maximum(x, 0.0) + jnp.log(1.0 + jnp.exp(-jnp.abs(x)))


def _silu(x):
    return x * jax.nn.sigmoid(x)


def _mod_kernel(c_ref, w_ref, b_ref, o_ref):
    o_ref[0] = _dot_hp(_silu(c_ref[...]), w_ref[0]) + b_ref[0]


def _mod_call(cvec, w_mod, b_mod):
    depth, d, n = w_mod.shape
    rows = cvec.shape[0]
    tn = 1536
    return pl.pallas_call(
        _mod_kernel,
        grid=(depth, n // tn),
        in_specs=[pl.BlockSpec((rows, d), lambda l, j: (0, 0)),
                  pl.BlockSpec((1, d, tn), lambda l, j: (l, 0, j)),
                  pl.BlockSpec((1, 1, tn), lambda l, j: (l, 0, j))],
        out_specs=pl.BlockSpec((1, rows, tn), lambda l, j: (l, 0, j)),
        out_shape=jax.ShapeDtypeStruct((depth, rows, n), F32),
        compiler_params=_cparams("parallel", "parallel"),
        name="mod",
    )(cvec, w_mod, b_mod.reshape(depth, 1, n))


def _inproj_kernel(x_ref, sh_ref, sc_ref, w_ref, wdt_ref, wdtT_ref, o_ref, dt_ref, dtT_ref):
    h = _layer_norm(x_ref[0]) * (1.0 + sc_ref[0]) + sh_ref[0]
    hb = h.astype(BF16)
    step = 512
    for c0 in range(0, MAIN_COLS, step):
        o_ref[0, :, c0:c0 + step] = _dot(hb, w_ref[:, c0:c0 + step]).astype(o_ref.dtype)
    dt_ref[0] = _dot_hp(h, wdt_ref[...])
    dtT_ref[0] = _dot_hp_nt(wdtT_ref[...], h)


def _inproj_call(x, shift, scale, w_main, w_dt, w_dtT, tm):
    b, t, d = x.shape
    return pl.pallas_call(
        _inproj_kernel,
        grid=(b, t // tm),
        in_specs=[pl.BlockSpec((1, tm, d), lambda bi, i: (bi, i, 0)),
                  pl.BlockSpec((1, 1, d), lambda bi, i: (bi, 0, 0)),
                  pl.BlockSpec((1, 1, d), lambda bi, i: (bi, 0, 0)),
                  pl.BlockSpec((d, MAIN_COLS), lambda bi, i: (0, 0)),
                  pl.BlockSpec((d, LANES), lambda bi, i: (0, 0)),
                  pl.BlockSpec((2 * SSD_HEADS, d), lambda bi, i: (0, 0))],
        out_specs=[pl.BlockSpec((1, tm, MAIN_COLS), lambda bi, i: (bi, i, 0)),
                   pl.BlockSpec((1, tm, LANES), lambda bi, i: (bi, i, 0)),
                   pl.BlockSpec((1, 2 * SSD_HEADS, tm), lambda bi, i: (bi, 0, i))],
        out_shape=[jax.ShapeDtypeStruct((b, t, MAIN_COLS), BF16),
                   jax.ShapeDtypeStruct((b, t, LANES), F32),
                   jax.ShapeDtypeStruct((b, 2 * SSD_HEADS, t), F32)],
        compiler_params=_cparams("parallel", "parallel"),
        name="inproj",
    )(x, shift, scale, w_main, w_dt, w_dtT)


def _head_rms(x, bd, g):
    hi, lo = _split2(x * x)
    ss = _dot(hi, bd) + _dot(lo, bd)
    return x * lax.rsqrt(ss * (1.0 / HEAD_DIM) + EPS) * g


def _rope(x, cos, sin_signed):
    half = AXIS_ROT // 2
    outs = []
    for j in range(x.shape[1] // LANES):
        xj = x[:, j * LANES:(j + 1) * LANES]
        lane = lax.broadcasted_iota(I32, xj.shape, 1)
        partner = jnp.where((lane & half) == 0, pltpu.roll(xj, LANES - half, 1), pltpu.roll(xj, half, 1))
        outs.append(xj * cos + partner * sin_signed)
    return outs[0] if len(outs) == 1 else jnp.concatenate(outs, axis=1)


def _qkprep_kernel(q_ref, kv_ref, cos_ref, sin_ref, gq_ref, gk_ref, bd_ref, qo_ref, ko_ref, vo_ref, *, rope):
    q = _head_rms(q_ref[0].astype(F32), bd_ref[...], gq_ref[...])
    kv = kv_ref[0]
    k = _head_rms(kv[:, :KV_WIDTH].astype(F32), bd_ref[:KV_WIDTH, :KV_WIDTH], gk_ref[...])
    if rope:
        q = _rope(q, cos_ref[...], sin_ref[...])
        k = _rope(k, cos_ref[...], sin_ref[...])
    q = q * (HEAD_DIM ** -0.5)
    for h in range(ATTN_HEADS):
        qo_ref[0, h] = q[:, h * HEAD_DIM:(h + 1) * HEAD_DIM].astype(BF16)
    v = kv[:, KV_WIDTH:].astype(F32)
    lane = lax.broadcasted_iota(I32, v.shape, 1)
    ones_col = jnp.where(lane == HEAD_DIM, 1.0, 0.0)
    for g in range(KV_HEADS):
        ko_ref[0, g] = k[:, g * HEAD_DIM:(g + 1) * HEAD_DIM].astype(BF16)
        vg = v if g == 0 else pltpu.roll(v, LANES - g * HEAD_DIM, 1)
        vo_ref[0, g] = jnp.where(lane < HEAD_DIM, vg, ones_col).astype(BF16)


def _qkprep_call(main, cos, sin, gq, gk, bd, tm, rope):
    b, t, _ = main.shape
    return pl.pallas_call(
        functools.partial(_qkprep_kernel, rope=rope),
        grid=(b, t // tm),
        in_specs=[pl.BlockSpec((1, tm, ATTN_WIDTH), lambda bi, i: (bi, i, COL_Q // ATTN_WIDTH)),
                  pl.BlockSpec((1, tm, 2 * KV_WIDTH), lambda bi, i: (bi, i, COL_KV // (2 * KV_WIDTH))),
                  pl.BlockSpec((tm, LANES), lambda bi, i: (i, 0)),
                  pl.BlockSpec((tm, LANES), lambda bi, i: (i, 0)),
                  pl.BlockSpec((1, ATTN_WIDTH), lambda bi, i: (0, 0)),
                  pl.BlockSpec((1, KV_WIDTH), lambda bi, i: (0, 0)),
                  pl.BlockSpec((ATTN_WIDTH, ATTN_WIDTH), lambda bi, i: (0, 0))],
        out_specs=[pl.BlockSpec((1, ATTN_HEADS, tm, HEAD_DIM), lambda bi, i: (bi, 0, i, 0)),
                   pl.BlockSpec((1, KV_HEADS, tm, HEAD_DIM), lambda bi, i: (bi, 0, i, 0)),
                   pl.BlockSpec((1, KV_HEADS, tm, LANES), lambda bi, i: (bi, 0, i, 0))],
        out_shape=[jax.ShapeDtypeStruct((b, ATTN_HEADS, t, HEAD_DIM), BF16),
                   jax.ShapeDtypeStruct((b, KV_HEADS, t, HEAD_DIM), BF16),
                   jax.ShapeDtypeStruct((b, KV_HEADS, t, LANES), BF16)],
        compiler_params=_cparams("parallel", "parallel"),
        name="qkprep",
    )(main, main, cos, sin, gq, gk, bd)


def _attn_kernel(q_ref, k_ref, v_ref, o_ref):
    k = k_ref[0, 0]
    v = v_ref[0, 0]
    outs = []
    for r in range(ATTN_REP):
        s = lax.dot_general(q_ref[0, r], k, _NT, preferred_element_type=F32)
        p = jnp.exp(s - jnp.max(s, axis=-1, keepdims=True)).astype(BF16)
        o = _dot(p, v)
        outs.append(o[:, :HEAD_DIM] / o[:, HEAD_DIM:HEAD_DIM + 1])
    o_ref[0] = jnp.concatenate(outs, axis=1).astype(o_ref.dtype)


def _attn_call(qh, kh, vh, tq):
    b, _, t, _ = qh.shape
    tk = kh.shape[2]
    return pl.pallas_call(
        _attn_kernel,
        grid=(b, KV_HEADS, t // tq),
        in_specs=[pl.BlockSpec((1, ATTN_REP, tq, HEAD_DIM), lambda bi, g, i: (bi, g, i, 0)),
                  pl.BlockSpec((1, 1, tk, HEAD_DIM), lambda bi, g, i: (bi, g, 0, 0)),
                  pl.BlockSpec((1, 1, tk, LANES), lambda bi, g, i: (bi, g, 0, 0))],
        out_specs=pl.BlockSpec((1, tq, ATTN_REP * HEAD_DIM), lambda bi, g, i: (bi, i, g)),
        out_shape=jax.ShapeDtypeStruct((b, t, ATTN_WIDTH), BF16),
        compiler_params=_cparams("parallel", "parallel", "parallel"),
        name="attn",
    )(qh, kh, vh)


def _conv3(u, prev_row, next_row, w):
    tm = u.shape[0]
    rid = lax.broadcasted_iota(I32, u.shape, 0)
    up = jnp.where(rid == 0, prev_row, pltpu.roll(u, 1, 0))
    dn = jnp.where(rid == tm - 1, next_row, pltpu.roll(u, tm - 1, 0))
    return up * w[0:1, :] + u * w[1:2, :] + dn * w[2:3, :]


def _conv_kernel(x_ref, xp_ref, xn_ref, c_ref, cp_ref, cn_ref, s_ref, sp_ref, sn_ref, b_ref,
                 w_ref, bias_ref, wsc_ref, xo_ref, yo_ref):
    i, n = pl.program_id(1), pl.num_programs(1)
    keep_p = jnp.where(i > 0, 1.0, 0.0)
    keep_n = jnp.where(i < n - 1, 1.0, 0.0)
    last = BF16_ROWS - 1
    f = lambda r: r.astype(F32)
    u = f(x_ref[0])
    y = _conv3(u, f(xp_ref[0, last:last + 1, :]) * keep_p, f(xn_ref[0, 0:1, :]) * keep_n, w_ref[...]) + bias_ref[...]
    xo_ref[0] = _silu(y).astype(xo_ref.dtype)
    cs = f(c_ref[0]) * f(s_ref[0])
    cs_p = f(cp_ref[0, last:last + 1, :]) * f(sp_ref[0, last:last + 1, :]) * keep_p
    cs_n = f(cn_ref[0, 0:1, :]) * f(sn_ref[0, 0:1, :]) * keep_n
    yo_ref[0] = (f(b_ref[0]) * _conv3(cs, cs_p, cs_n, wsc_ref[...])).astype(yo_ref.dtype)


def _conv_call(main, conv_w, conv_b, sc_w, tm):
    b, t, _ = main.shape
    hb = tm // BF16_ROWS
    nh = t // BF16_ROWS

    def centre(width, col):
        return pl.BlockSpec((1, tm, width), lambda bi, i: (bi, i, col // width))

    def prev(width, col):
        return pl.BlockSpec((1, BF16_ROWS, width), lambda bi, i: (bi, jnp.maximum(i * hb - 1, 0), col // width))

    def nxt(width, col):
        return pl.BlockSpec((1, BF16_ROWS, width), lambda bi, i: (bi, jnp.minimum((i + 1) * hb, nh - 1), col // width))

    const = lambda shape: pl.BlockSpec(shape, lambda bi, i: (0, 0))
    return pl.pallas_call(
        _conv_kernel,
        grid=(b, t // tm),
        in_specs=[centre(SSD_CONV_DIM, COL_XBC), prev(SSD_CONV_DIM, COL_XBC), nxt(SSD_CONV_DIM, COL_XBC),
                  centre(SC_WIDTH, COL_SCC), prev(SC_WIDTH, COL_SCC), nxt(SC_WIDTH, COL_SCC),
                  centre(SC_WIDTH, COL_SCX), prev(SC_WIDTH, COL_SCX), nxt(SC_WIDTH, COL_SCX),
                  centre(SC_WIDTH, COL_SCB),
                  const((3, SSD_CONV_DIM)), const((1, SSD_CONV_DIM)), const((3, SC_WIDTH))],
        out_specs=[pl.BlockSpec((1, tm, SSD_CONV_DIM), lambda bi, i: (bi, i, 0)),
                   pl.BlockSpec((1, tm, SC_WIDTH), lambda bi, i: (bi, i, 0))],
        out_shape=[jax.ShapeDtypeStruct((b, t, SSD_CONV_DIM), BF16),
                   jax.ShapeDtypeStruct((b, t, SC_WIDTH), BF16)],
        compiler_params=_cparams("parallel", "parallel"),
        name="conv",
    )(main, main, main, main, main, main, main, main, main, main, conv_w, conv_b, sc_w)


def _ssd_kernel(xa_ref, dt_ref, dtT_ref, z_ref, s0_ref, brow_ref, arow_ref, bcol_ref, acol_ref, dexp_ref, nw_ref,
                y_ref, so_ref, yf_ref, st_ref):
    L = SSD_CHUNK
    t = xa_ref.shape[1]
    nc = t // L
    ri = lax.broadcasted_iota(I32, (L, L), 0)
    ci = lax.broadcasted_iota(I32, (L, L), 1)
    lane512 = lax.broadcasted_iota(I32, (L, SSD_INNER), 1)
    row512 = lax.broadcasted_iota(I32, (L, SSD_INNER), 0)
    blockmask = jnp.where((row512 >> 6) == (lane512 >> 8), 1.0, 0.0)
    lane128 = lax.broadcasted_iota(I32, (L, LANES), 1)
    a_row = -jnp.exp(arow_ref[...])
    a_col = -jnp.exp(acol_ref[...])

    st_ref[...] = s0_ref[0]

    def chunk(c, d):
        r0 = pl.multiple_of(c * L, L)
        causal = (ri >= ci) if d == 0 else (ri <= ci)
        tri = jnp.where(causal, 1.0, 0.0).astype(BF16)
        triT = jnp.where((ri <= ci) if d == 0 else (ri >= ci), 1.0, 0.0).astype(BF16)
        expand = jnp.where(row512 == (lane512 >> 6) + d * SSD_HEADS, 1.0, 0.0).astype(BF16)
        xa = xa_ref[0, pl.ds(r0, L), :]
        x = xa[:, :SSD_INNER]
        b_pair = xa[:, SSD_INNER:SSD_INNER + SSD_GN]
        c_pair = xa[:, SSD_INNER + SSD_GN:]
        dt_all = _softplus(dt_ref[0, pl.ds(r0, L), :] + brow_ref[...])
        a1, a2, a3 = _split3(dt_all * a_row)
        acs = _dot(tri, a1) + _dot(tri, a2) + _dot(tri, a3)
        dtT = _softplus(dtT_ref[0, :, pl.ds(r0, L)] + bcol_ref[...])
        acsT = _dot_exact_lhs(dtT * a_col, triT)
        edge = acs[L - 1:L, :] if d == 0 else acs[0:1, :]
        w_all = jnp.exp(edge - acs) * dt_all
        e_exp = jnp.exp(_dot_exact_lhs(acs, expand))
        w_exp = _dot_exact_lhs(w_all, expand)
        state = st_ref[d]
        y_off = _dot(c_pair, state.astype(BF16)) * e_exp
        gmats = [lax.dot_general(jnp.where((lane128 >> 6) == g, c_pair, jnp.zeros_like(c_pair)), b_pair, _NT,
                                 preferred_element_type=F32) for g in range(SSD_GROUPS)]
        pairs = []
        for pr in range(SSD_HEADS // 2):
            x_pair = x[:, pr * LANES:(pr + 1) * LANES]
            acc = jnp.zeros((L, LANES), F32)
            for hh in range(2):
                h = 2 * pr + hh
                gmat = gmats[h // (SSD_HEADS // SSD_GROUPS)]
                k = d * SSD_HEADS + h
                seg = acs[:, k:k + 1] - acsT[k:k + 1, :]
                m = jnp.where(causal, jnp.exp(jnp.where(causal, seg, 0.0)), 0.0) * gmat * dtT[k:k + 1, :]
                xh = jnp.where((lane128 >> 6) == hh, x_pair, jnp.zeros_like(x_pair))
                acc = acc + _dot(m.astype(BF16), xh)
            pairs.append(acc)
        y = y_off + jnp.concatenate(pairs, axis=1)
        e_edge = e_exp[L - 1:L, :] if d == 0 else e_exp[0:1, :]
        ds = lax.dot_general(b_pair, (x.astype(F32) * w_exp).astype(BF16), _TN, preferred_element_type=F32)
        st_ref[d] = (state * e_edge + ds) * blockmask
        return y, x, r0

    def fwd(c, carry):
        y, _, r0 = chunk(c, 0)
        yf_ref[pl.ds(r0, L), :] = y
        return carry

    lax.fori_loop(0, nc, fwd, 0)

    def bwd(i, carry):
        c = nc - 1 - i
        yb, x, r0 = chunk(c, 1)
        y = yf_ref[pl.ds(r0, L), :] + yb + dexp_ref[...] * x.astype(F32)
        gated = y * _silu(z_ref[0, pl.ds(r0, L), :].astype(F32))
        ms = jnp.mean(gated * gated, axis=-1, keepdims=True)
        y_ref[0, pl.ds(r0, L), :] = (gated * lax.rsqrt(ms + EPS) * nw_ref[...]).astype(y_ref.dtype)
        return carry

    lax.fori_loop(0, nc, bwd, 0)
    so_ref[0] = st_ref[...]


def _ssd_call(xa, dt, dtT, main, s0, brow, arow, bcol, acol, dexp, nw):
    b, t, _ = xa.shape
    c2 = lambda shape: pl.BlockSpec(shape, lambda bi: (0, 0))
    return pl.pallas_call(
        _ssd_kernel,
        grid=(b,),
        in_specs=[pl.BlockSpec((1, t, SSD_CONV_DIM), lambda bi: (bi, 0, 0)),
                  pl.BlockSpec((1, t, LANES), lambda bi: (bi, 0, 0)),
                  pl.BlockSpec((1, 2 * SSD_HEADS, t), lambda bi: (bi, 0, 0)),
                  pl.BlockSpec((1, t, SSD_INNER), lambda bi: (bi, 0, COL_Z // SSD_INNER)),
                  pl.BlockSpec((1, 2, SSD_GN, SSD_INNER), lambda bi: (bi, 0, 0, 0)),
                  c2((1, LANES)), c2((1, LANES)), c2((2 * SSD_HEADS, LANES)), c2((2 * SSD_HEADS, LANES)),
                  c2((1, SSD_INNER)), c2((1, SSD_INNER))],
        out_specs=[pl.BlockSpec((1, t, SSD_INNER), lambda bi: (bi, 0, 0)),
                   pl.BlockSpec((1, 2, SSD_GN, SSD_INNER), lambda bi: (bi, 0, 0, 0))],
        out_shape=[jax.ShapeDtypeStruct((b, t, SSD_INNER), BF16),
                   jax.ShapeDtypeStruct((b, 2, SSD_GN, SSD_INNER), F32)],
        scratch_shapes=[pltpu.VMEM((t, SSD_INNER), F32), pltpu.VMEM((2, SSD_GN, SSD_INNER), F32)],
        compiler_params=_cparams("parallel"),
        name="ssd",
    )(xa, dt, dtT, main, s0, brow, arow, bcol, acol, dexp, nw)


def _merge_kernel(ya_ref, ys_ref, yc_ref, g_ref, x_ref, m2_ref, m3_ref, m4_ref, wa_ref, ws_ref, wc_ref, wo_ref,
                  lg_ref, lb_ref, wr_ref, x1_ref, h2_ref, aff_ref, *, alpha):
    d = x_ref.shape[2]
    gate = lambda j: jax.nn.sigmoid(g_ref[0, :, j * d:(j + 1) * d].astype(F32))
    m = (gate(0) * _dot(ya_ref[0], wa_ref[...]) + gate(1) * _dot(ys_ref[0], ws_ref[...])
         + gate(2) * _dot(yc_ref[0], wc_ref[...]))
    out = _dot(m.astype(BF16), wo_ref[...])
    x1 = _layer_norm(alpha * x_ref[0] + m2_ref[0] * out) * lg_ref[...] + lb_ref[...]
    x1_ref[0] = x1
    h2 = _layer_norm(x1) * (1.0 + m4_ref[0]) + m3_ref[0]
    h2_ref[0] = h2.astype(h2_ref.dtype)
    logits = _dot_hp_nt(wr_ref[...], h2)
    e = jnp.exp(logits - jnp.max(logits, axis=0, keepdims=True))
    aff_ref[0] = e / jnp.sum(e, axis=0, keepdims=True)


def _merge_call(ya, ys, yc, main, x, m2, m3, m4, wa, ws, wc, wo, lg, lb, wrT, alpha, tm):
    b, t, d = x.shape
    tok = lambda w: pl.BlockSpec((1, tm, w), lambda bi, i: (bi, i, 0))
    per_b = pl.BlockSpec((1, 1, d), lambda bi, i: (bi, 0, 0))
    const = lambda shape: pl.BlockSpec(shape, lambda bi, i: (0, 0))
    return pl.pallas_call(
        functools.partial(_merge_kernel, alpha=alpha),
        grid=(b, t // tm),
        in_specs=[tok(ATTN_WIDTH), tok(SSD_INNER), tok(SC_WIDTH), tok(N_BRANCH * d), tok(d),
                  per_b, per_b, per_b,
                  const((ATTN_WIDTH, d)), const((SSD_INNER, d)), const((SC_WIDTH, d)), const((d, d)),
                  const((1, d)), const((1, d)), const((N_EXPERTS, d))],
        out_specs=[tok(d), tok(d), pl.BlockSpec((1, N_EXPERTS, tm), lambda bi, i: (bi, 0, i))],
        out_shape=[jax.ShapeDtypeStruct((b, t, d), F32), jax.ShapeDtypeStruct((b, t, d), BF16),
                   jax.ShapeDtypeStruct((b, N_EXPERTS, t), F32)],
        compiler_params=_cparams("parallel", "parallel"),
        name="merge",
    )(ya, ys, yc, main, x, m2, m3, m4, wa, ws, wc, wo, lg, lb, wrT)


def _route_kernel(aff_ref, pos_ref, w_ref, offs_ref, *, cap):
    aff = aff_ref[0]
    ne, t = aff.shape
    capf = float(cap)

    def step(i, lo):
        cand = lo | jnp.left_shift(jnp.int32(1), 30 - i)
        cnt = jnp.sum(jnp.where(aff >= pltpu.bitcast(cand, F32), 1.0, 0.0), axis=1, keepdims=True)
        return jnp.where(cnt >= capf, cand, lo)

    thr = pltpu.bitcast(lax.fori_loop(0, 31, step, jnp.zeros((ne, 1), I32)), F32)
    need = capf - jnp.sum(jnp.where(aff > thr, 1.0, 0.0), axis=1, keepdims=True)
    ri = lax.broadcasted_iota(I32, (LANES, LANES), 0)
    ci = lax.broadcasted_iota(I32, (LANES, LANES), 1)
    upper = jnp.where(ri < ci, 1.0, 0.0).astype(BF16)
    lane = lax.broadcasted_iota(I32, (ne, LANES), 1)
    carry_gt = jnp.zeros((ne, 1), F32)
    carry_eq = jnp.zeros((ne, 1), F32)
    offs = jnp.zeros((ne, LANES), F32)
    nblk = t // LANES
    for j in range(nblk):
        sl = slice(j * LANES, (j + 1) * LANES)
        aj = aff[:, sl]
        gj = jnp.where(aj > thr, 1.0, 0.0)
        ej = jnp.where(aj == thr, 1.0, 0.0)
        pre = _dot(jnp.concatenate([gj, ej], axis=0).astype(BF16), upper)
        pre_gt = pre[:ne] + carry_gt
        pre_eq = pre[ne:] + carry_eq
        sel = gj + ej * jnp.where(pre_eq < need, 1.0, 0.0)
        slot = pre_gt + jnp.minimum(pre_eq, need)
        pos_ref[0, :, sl] = jnp.where(sel > 0.5, slot, -1.0)
        w_ref[0, :, sl] = jnp.where(sel > 0.5, aj, 0.0)
        offs = jnp.where(lane == j, carry_gt + jnp.minimum(carry_eq, need), offs)
        carry_gt = carry_gt + jnp.sum(gj, axis=1, keepdims=True)
        carry_eq = carry_eq + jnp.sum(ej, axis=1, keepdims=True)
    offs = jnp.where(lane >= nblk, capf, offs)
    offs_ref[0] = offs.astype(I32)


def _route_call(affT, cap):
    b, ne, t = affT.shape
    spec = pl.BlockSpec((1, ne, t), lambda bi: (bi, 0, 0))
    return pl.pallas_call(
        functools.partial(_route_kernel, cap=cap),
        grid=(b,),
        in_specs=[spec],
        out_specs=[spec, spec, pl.BlockSpec((1, ne, LANES), lambda bi: (bi, 0, 0))],
        out_shape=[jax.ShapeDtypeStruct((b, ne, t), F32), jax.ShapeDtypeStruct((b, ne, t), F32),
                   jax.ShapeDtypeStruct((b, ne, LANES), I32)],
        compiler_params=_cparams("parallel"),
        name="route",
    )(affT)


def _gather_kernel(offs_ref, h_ref, pos_ref, xs_ref, acc_ref, *, ts, tb):
    bi, e = pl.program_id(0), pl.program_id(1)
    t = h_ref.shape[1]
    cap = xs_ref.shape[2]
    per = tb // LANES
    pos_row = pos_ref[0, pl.ds(e, 1), :]
    base = (bi * pos_ref.shape[1] + e) * LANES
    acc_ref[...] = jnp.zeros_like(acc_ref)
    for s in range(cap // ts):
        slot = (lax.broadcasted_iota(I32, (ts, tb), 0) + s * ts).astype(F32)
        for j in range(t // tb):
            lo = offs_ref[base + j * per]
            hi = offs_ref[base + (j + 1) * per]

            @pl.when((hi > s * ts) & (lo < (s + 1) * ts))
            def _():
                onehot = jnp.where(pos_row[:, j * tb:(j + 1) * tb] == slot, 1.0, 0.0).astype(BF16)
                acc_ref[s * ts:(s + 1) * ts, :] += _dot(onehot, h_ref[0, j * tb:(j + 1) * tb, :])
    xs_ref[0, 0] = acc_ref[...].astype(xs_ref.dtype)


def _gather_call(offs, h2, pos, cap, ts, tb):
    b, t, d = h2.shape
    ne = pos.shape[1]
    return pl.pallas_call(
        functools.partial(_gather_kernel, ts=ts, tb=tb),
        grid_spec=pltpu.PrefetchScalarGridSpec(
            num_scalar_prefetch=1,
            grid=(b, ne),
            in_specs=[pl.BlockSpec((1, t, d), lambda bi, e, offs: (bi, 0, 0)),
                      pl.BlockSpec((1, ne, t), lambda bi, e, offs: (bi, 0, 0))],
            out_specs=pl.BlockSpec((1, 1, cap, d), lambda bi, e, offs: (bi, e, 0, 0)),
            scratch_shapes=[pltpu.VMEM((cap, d), F32)]),
        out_shape=jax.ShapeDtypeStruct((b, ne, cap, d), BF16),
        compiler_params=_cparams("parallel", "arbitrary"),
        name="gather",
    )(offs, h2, pos)


def _ffn_kernel(*refs, n_sets):
    xs_refs, (wg_ref, wu_ref, wd_ref) = refs[:n_sets], refs[n_sets:n_sets + 3]
    o_refs, acc_refs = refs[n_sets + 3:2 * n_sets + 3], refs[2 * n_sets + 3:]
    f, nf = pl.program_id(1), pl.num_programs(1)
    wg = wg_ref[0, 0].astype(BF16)
    wu = wu_ref[0, 0].astype(BF16)
    wd = wd_ref[0, 0].astype(BF16)

    @pl.when(f == 0)
    def _():
        for acc_ref in acc_refs:
            acc_ref[...] = jnp.zeros_like(acc_ref)

    for xs_ref, acc_ref in zip(xs_refs, acc_refs):
        for bi in range(xs_ref.shape[0]):
            x = xs_ref[bi, 0]
            hid = (_silu(_dot(x, wg)) * _dot(x, wu)).astype(BF16)
            acc_ref[bi] += _dot(hid, wd)

    @pl.when(f == nf - 1)
    def _():
        for o_ref, acc_ref in zip(o_refs, acc_refs):
            o_ref[:, 0] = acc_ref[...].astype(o_ref.dtype)


def _ffn_call(xs_sets, wg, wu, wd, layer, tf):
    ne, d = xs_sets[0].shape[1], xs_sets[0].shape[3]
    ff = wg.shape[3]
    tok = lambda xs: pl.BlockSpec((xs.shape[0], 1, xs.shape[2], d), lambda e, f: (0, e, 0, 0))
    return pl.pallas_call(
        functools.partial(_ffn_kernel, n_sets=len(xs_sets)),
        grid=(ne, ff // tf),
        in_specs=[tok(xs) for xs in xs_sets]
        + [pl.BlockSpec((1, 1, d, tf), lambda e, f: (layer, e, 0, f)),
           pl.BlockSpec((1, 1, d, tf), lambda e, f: (layer, e, 0, f)),
           pl.BlockSpec((1, 1, tf, d), lambda e, f: (layer, e, f, 0))],
        out_specs=[tok(xs) for xs in xs_sets],
        out_shape=[jax.ShapeDtypeStruct(xs.shape, BF16) for xs in xs_sets],
        scratch_shapes=[pltpu.VMEM((xs.shape[0], xs.shape[2], d), F32) for xs in xs_sets],
        compiler_params=_cparams("parallel", "arbitrary"),
        name="ffn",
    )(*xs_sets, wg, wu, wd)


def _combine_kernel(offs_ref, o_ref, pos_ref, w_ref, x_ref, m5_ref, lg_ref, lb_ref, y_ref, acc_ref, *, alpha, ts):
    bi, i = pl.program_id(0), pl.program_id(1)
    tm = x_ref.shape[1]
    ne, cap = o_ref.shape[1], o_ref.shape[2]
    per = tm // LANES
    pos = pos_ref[0]
    w = w_ref[0]
    acc_ref[...] = jnp.zeros_like(acc_ref)
    lo_all = [offs_ref[(bi * ne + e) * LANES + i * per] for e in range(ne)]
    hi_all = [offs_ref[(bi * ne + e) * LANES + (i + 1) * per] for e in range(ne)]
    for e in range(ne):
        for s in range(cap // ts):
            @pl.when((hi_all[e] > s * ts) & (lo_all[e] < (s + 1) * ts))
            def _():
                slot = (lax.broadcasted_iota(I32, (tm, ts), 1) + s * ts).astype(F32)
                onehot = jnp.where(pos[:, e:e + 1] == slot, w[:, e:e + 1], 0.0).astype(BF16)
                acc_ref[...] += _dot(onehot, o_ref[0, e, s * ts:(s + 1) * ts, :])
    y_ref[0] = _layer_norm(alpha * x_ref[0] + m5_ref[0] * acc_ref[...]) * lg_ref[...] + lb_ref[...]


def _combine_call(offs, out, posT, wT, x1, m5, lg, lb, alpha, tm, ts):
    b, t, d = x1.shape
    ne, cap = out.shape[1], out.shape[2]
    return pl.pallas_call(
        functools.partial(_combine_kernel, alpha=alpha, ts=ts),
        grid_spec=pltpu.PrefetchScalarGridSpec(
            num_scalar_prefetch=1,
            grid=(b, t // tm),
            in_specs=[pl.BlockSpec((1, ne, cap, d), lambda bi, i, offs: (bi, 0, 0, 0)),
                      pl.BlockSpec((1, tm, ne), lambda bi, i, offs: (bi, i, 0)),
                      pl.BlockSpec((1, tm, ne), lambda bi, i, offs: (bi, i, 0)),
                      pl.BlockSpec((1, tm, d), lambda bi, i, offs: (bi, i, 0)),
                      pl.BlockSpec((1, 1, d), lambda bi, i, offs: (bi, 0, 0)),
                      pl.BlockSpec((1, d), lambda bi, i, offs: (0, 0)),
                      pl.BlockSpec((1, d), lambda bi, i, offs: (0, 0))],
            out_specs=pl.BlockSpec((1, tm, d), lambda bi, i, offs: (bi, i, 0)),
            scratch_shapes=[pltpu.VMEM((tm, d), F32)]),
        out_shape=jax.ShapeDtypeStruct((b, t, d), F32),
        compiler_params=_cparams("parallel", "arbitrary"),
        name="combine",
    )(offs, out, posT, wT, x1, m5, lg, lb)


def _rope_tables(n):
    rows = n // GRID_W
    row = jnp.repeat(jnp.arange(rows), GRID_W).astype(F32)
    col = jnp.tile(jnp.arange(GRID_W), rows).astype(F32)
    inv = ROPE_THETA ** (-jnp.arange(0, AXIS_ROT, 2, dtype=F32) / AXIS_ROT)
    ang = jnp.stack([row[:, None] * inv, col[:, None] * inv], axis=1)
    cos = jnp.repeat(jnp.cos(ang)[:, :, None, :], 2, axis=2).reshape(n, HEAD_DIM)
    sin = jnp.stack([-jnp.sin(ang), jnp.sin(ang)], axis=2).reshape(n, HEAD_DIM)
    return jnp.tile(cos, (1, LANES // HEAD_DIM)), jnp.tile(sin, (1, LANES // HEAD_DIM))


def _tile(t, pref):
    return pref if t % pref == 0 else t


def _moe_dispatch(h2, affT):
    t = h2.shape[1]
    cap = CAPACITY_FACTOR * t // N_EXPERTS
    pos, wsel, offs = _route_call(affT, cap)
    offs = offs.reshape(-1)
    xs = _gather_call(offs, h2, pos, cap, ts=_tile(cap, 128), tb=_tile(t, 256))
    return xs, (offs, jnp.swapaxes(pos, 1, 2), jnp.swapaxes(wsel, 1, 2))


def _moe_combine(out, routing, x1, m5, lg, lb, alpha):
    offs, posT, wT = routing
    t, cap = x1.shape[1], out.shape[2]
    return _combine_call(offs, out, posT, wT, x1, m5, lg, lb, alpha, tm=_tile(t, 512), ts=_tile(cap, 256))


def kernel(x, c, ctx, c_ctx, w_mod, b_mod, w_in, q_norm, k_norm, ssd_conv_w, ssd_conv_b, ssd_a_log, ssd_dt_bias,
           ssd_d, ssd_norm, sc_conv_w, w_br_attn, w_br_ssd, w_br_conv, w_o, ln1_g, ln1_b, w_router, w_exp_gate,
           w_exp_up, w_exp_down, ln2_g, ln2_b):
    b, n, d = x.shape
    nctx = ctx.shape[1]
    depth = w_mod.shape[0]
    alpha = (2 * depth) ** 0.25

    rows = -(-(b + 1) // 8) * 8
    cvec = jnp.concatenate([c, c_ctx[None], jnp.zeros((rows - b - 1, d), F32)], axis=0)
    mods = _mod_call(cvec, w_mod, b_mod)

    cos_l, sin_l = _rope_tables(n)
    cos_c, sin_c = jnp.ones((nctx, LANES), F32), jnp.zeros((nctx, LANES), F32)
    hid = jnp.arange(ATTN_WIDTH) // HEAD_DIM
    bd = (hid[:, None] == hid[None, :]).astype(BF16)
    dexp_all = jnp.repeat(ssd_d, SSD_HEAD_DIM, axis=1)
    pad16 = lambda v: jnp.pad(v.reshape(1, 2 * SSD_HEADS), ((0, 0), (0, LANES - 2 * SSD_HEADS)))
    col16 = lambda v: jnp.broadcast_to(v.reshape(2 * SSD_HEADS, 1), (2 * SSD_HEADS, LANES))

    x_ctx = ctx
    for i in range(depth):
        need_ctx = i < depth - 1
        wi = w_in[i]
        k_, v_, xbc_, dt_, q_, z_, scb_, scc_, scx_, gate_ = jnp.split(
            wi, [128, 256, 1024, 1040, 1552, 2064, 2576, 3088, 3600], axis=1)
        w_main = jnp.concatenate([gate_, xbc_, k_, v_, q_, z_, scb_, scc_, scx_], axis=1).astype(BF16)
        w_dt = jnp.pad(dt_, ((0, 0), (0, LANES - 2 * SSD_HEADS)))
        w_dtT = dt_.T
        gq = jnp.tile(q_norm[i], ATTN_HEADS)[None]
        gk = jnp.tile(k_norm[i], KV_HEADS)[None]
        brow, arow = pad16(ssd_dt_bias[i]), pad16(ssd_a_log[i])
        bcol, acol = col16(ssd_dt_bias[i]), col16(ssd_a_log[i])
        dexp, nw = dexp_all[i][None], ssd_norm[i][None]
        wa, ws, wc, wo = (w.astype(BF16) for w in (w_br_attn[i], w_br_ssd[i], w_br_conv[i], w_o[i]))
        wrT = w_router[i].T
        lg1, lb1, lg2, lb2 = ln1_g[i][None], ln1_b[i][None], ln2_g[i][None], ln2_b[i][None]
        mod_l = [m[:, None, :] for m in jnp.split(mods[i, :b], N_MOD, axis=-1)]
        mod_c = [jnp.broadcast_to(m[None, None, :], (b, 1, d)) for m in jnp.split(mods[i, b], N_MOD, axis=-1)]

        def mixer(xin, mod, cos, sin, rope, s0):
            t = xin.shape[1]
            main, dt, dtT = _inproj_call(xin, mod[0], mod[1], w_main, w_dt, w_dtT, tm=_tile(t, 512))
            qh, kh, vh = _qkprep_call(main, cos, sin, gq, gk, bd, tm=_tile(t, 512), rope=rope)
            xa, yc = _conv_call(main, ssd_conv_w[i], ssd_conv_b[i][None], sc_conv_w[i], tm=_tile(t, 512))
            ys, s_out = _ssd_call(xa, dt, dtT, main, s0, brow, arow, bcol, acol, dexp, nw)
            return main, qh, kh, vh, yc, ys, s_out

        def merge(xin, mod, main, ya, ys, yc):
            return _merge_call(ya, ys, yc, main, xin, mod[2], mod[3], mod[4], wa, ws, wc, wo, lg1, lb1, wrT,
                               alpha, tm=_tile(xin.shape[1], 256))

        zero_state = jnp.zeros((b, 2, SSD_GN, SSD_INNER), F32)
        main_c, qh_c, kh_c, vh_c, yc_c, ys_c, s_ctx = mixer(x_ctx, mod_c, cos_c, sin_c, False, zero_state)
        main_l, qh_l, kh_l, vh_l, yc_l, ys_l, _ = mixer(x, mod_l, cos_l, sin_l, True, s_ctx)
        keys = jnp.concatenate([kh_c, kh_l], axis=2)
        vals = jnp.concatenate([vh_c, vh_l], axis=2)
        ya_l = _attn_call(qh_l, keys, vals, tq=_tile(n, 256))
        streams = [(merge(x, mod_l, main_l, ya_l, ys_l, yc_l), mod_l)]
        if need_ctx:
            ya_c = _attn_call(qh_c, kh_c, vh_c, tq=_tile(nctx, 256))
            streams.append((merge(x_ctx, mod_c, main_c, ya_c, ys_c, yc_c), mod_c))
        dispatched = [_moe_dispatch(h2, affT) for (_, h2, affT), _ in streams]
        outs = _ffn_call([xs for xs, _ in dispatched], w_exp_gate, w_exp_up, w_exp_down, i,
                         tf=_tile(w_exp_gate.shape[3], 512))
        new = [_moe_combine(out, routing, x1, mod[5], lg2, lb2, alpha)
               for out, (_, routing), ((x1, _, _), mod) in zip(outs, dispatched, streams)]
        x = new[0]
        if need_ctx:
            x_ctx = new[1]
    return x
```

```python
import functools
import math

import jax
import jax.numpy as jnp
from jax import lax
from jax.experimental import pallas as pl
from jax.experimental.pallas import tpu as pltpu

F32, BF16, I32 = jnp.float32, jnp.bfloat16, jnp.int32

HEAD_DIM = 64
ATTN_HEADS = 8
KV_HEADS = 2
ATTN_REP = ATTN_HEADS // KV_HEADS
ATTN_WIDTH = ATTN_HEADS * HEAD_DIM
KV_WIDTH = KV_HEADS * HEAD_DIM
AXIS_ROT = HEAD_DIM // 2
ROPE_THETA = 10000.0
GRID_W = 64
SSD_HEADS = 8
SSD_HEAD_DIM = 64
SSD_INNER = SSD_HEADS * SSD_HEAD_DIM
SSD_STATE = 64
SSD_GROUPS = 2
SSD_GN = SSD_GROUPS * SSD_STATE
SSD_CONV_DIM = SSD_INNER + 2 * SSD_GN
SSD_CHUNK = 128
SC_WIDTH = 512
N_BRANCH = 3
N_EXPERTS = 16
CAPACITY_FACTOR = 2
N_MOD = 6
EPS = 1e-6

LANES = 128
BF16_ROWS = 16
VMEM_LIMIT_BYTES = 56 * 1024 * 1024

COL_GATE, COL_XBC, COL_KV, COL_Q, COL_Z, COL_SCB, COL_SCC, COL_SCX = 0, 3072, 3840, 4096, 4608, 5120, 5632, 6144
MAIN_COLS = 6656

_NT = (((1,), (1,)), ((), ()))
_TN = (((0,), (0,)), ((), ()))


def _cparams(*sem):
    return pltpu.CompilerParams(dimension_semantics=sem, vmem_limit_bytes=VMEM_LIMIT_BYTES)


def _dot(a, b):
    return jnp.dot(a, b, preferred_element_type=F32)


def _split2(a):
    hi = a.astype(BF16)
    lo = (a - hi.astype(F32)).astype(BF16)
    return hi, lo


def _split3(a):
    p1 = a.astype(BF16)
    r1 = a - p1.astype(F32)
    p2 = r1.astype(BF16)
    p3 = (r1 - p2.astype(F32)).astype(BF16)
    return p1, p2, p3


def _dot_hp(a, w):
    ah, al = _split2(a)
    wh, wl = _split2(w)
    return _dot(ah, wh) + _dot(al, wh) + _dot(ah, wl)


def _dot_hp_nt(a, w):
    ah, al = _split2(a)
    wh, wl = _split2(w)
    d = lambda x, y: lax.dot_general(x, y, _NT, preferred_element_type=F32)
    return d(ah, wh) + d(al, wh) + d(ah, wl)


def _dot_exact_lhs(a, m):
    p1, p2, p3 = _split3(a)
    return _dot(p1, m) + _dot(p2, m) + _dot(p3, m)


def _layer_norm(x):
    mu = jnp.mean(x, axis=-1, keepdims=True)
    xc = x - mu
    var = jnp.mean(xc * xc, axis=-1, keepdims=True)
    return xc * lax.rsqrt(var + EPS)


def _softplus(x):
    return jnp.maximum(x, 0.0) + jnp.log(1.0 + jnp.exp(-jnp.abs(x)))


def _silu(x):
    return x * jax.nn.sigmoid(x)


def _mod_kernel(c_ref, w_ref, b_ref, o_ref):
    o_ref[0] = _dot_hp(_silu(c_ref[...]), w_ref[0]) + b_ref[0]


def _mod_call(cvec, w_mod, b_mod):
    depth, d, n = w_mod.shape
    rows = cvec.shape[0]
    tn = 1536
    return pl.pallas_call(
        _mod_kernel,
        grid=(depth, n // tn),
        in_specs=[pl.BlockSpec((rows, d), lambda l, j: (0, 0)),
                  pl.BlockSpec((1, d, tn), lambda l, j: (l, 0, j)),
                  pl.BlockSpec((1, 1, tn), lambda l, j: (l, 0, j))],
        out_specs=pl.BlockSpec((1, rows, tn), lambda l, j: (l, 0, j)),
        out_shape=jax.ShapeDtypeStruct((depth, rows, n), F32),
        compiler_params=_cparams("parallel", "parallel"),
        name="mod",
    )(cvec, w_mod, b_mod.reshape(depth, 1, n))


def _inproj_kernel(x_ref, sh_ref, sc_ref, w_ref, wdt_ref, wdtT_ref, o_ref, dt_ref, dtT_ref):
    h = _layer_norm(x_ref[0]) * (1.0 + sc_ref[0]) + sh_ref[0]
    hb = h.astype(BF16)
    step = 512
    for c0 in range(0, MAIN_COLS, step):
        o_ref[0, :, c0:c0 + step] = _dot(hb, w_ref[:, c0:c0 + step]).astype(o_ref.dtype)
    dt_ref[0] = _dot_hp(h, wdt_ref[...])
    dtT_ref[0] = _dot_hp_nt(wdtT_ref[...], h)


def _inproj_call(x, shift, scale, w_main, w_dt, w_dtT, tm):
    b, t, d = x.shape
    return pl.pallas_call(
        _inproj_kernel,
        grid=(b, t // tm),
        in_specs=[pl.BlockSpec((1, tm, d), lambda bi, i: (bi, i, 0)),
                  pl.BlockSpec((1, 1, d), lambda bi, i: (bi, 0, 0)),
                  pl.BlockSpec((1, 1, d), lambda bi, i: (bi, 0, 0)),
                  pl.BlockSpec((d, MAIN_COLS), lambda bi, i: (0, 0)),
                  pl.BlockSpec((d, LANES), lambda bi, i: (0, 0)),
                  pl.BlockSpec((2 * SSD_HEADS, d), lambda bi, i: (0, 0))],
        out_specs=[pl.BlockSpec((1, tm, MAIN_COLS), lambda bi, i: (bi, i, 0)),
                   pl.BlockSpec((1, tm, LANES), lambda bi, i: (bi, i, 0)),
                   pl.BlockSpec((1, 2 * SSD_HEADS, tm), lambda bi, i: (bi, 0, i))],
        out_shape=[jax.ShapeDtypeStruct((b, t, MAIN_COLS), BF16),
                   jax.ShapeDtypeStruct((b, t, LANES), F32),
                   jax.ShapeDtypeStruct((b, 2 * SSD_HEADS, t), F32)],
        compiler_params=_cparams("parallel", "parallel"),
        name="inproj",
    )(x, shift, scale, w_main, w_dt, w_dtT)


def _head_rms(x, bd, g):
    hi, lo = _split2(x * x)
    ss = _dot(hi, bd) + _dot(lo, bd)
    return x * lax.rsqrt(ss * (1.0 / HEAD_DIM) + EPS) * g


def _rope(x, cos, sin_signed):
    half = AXIS_ROT // 2
    outs = []
    for j in range(x.shape[1] // LANES):
        xj = x[:, j * LANES:(j + 1) * LANES]
        lane = lax.broadcasted_iota(I32, xj.shape, 1)
        partner = jnp.where((lane & half) == 0, pltpu.roll(xj, LANES - half, 1), pltpu.roll(xj, half, 1))
        outs.append(xj * cos + partner * sin_signed)
    return outs[0] if len(outs) == 1 else jnp.concatenate(outs, axis=1)


def _qkprep_kernel(q_ref, kv_ref, cos_ref, sin_ref, gq_ref, gk_ref, bd_ref, qo_ref, ko_ref, vo_ref, *, rope):
    q = _head_rms(q_ref[0].astype(F32), bd_ref[...], gq_ref[...])
    kv = kv_ref[0]
    k = _head_rms(kv[:, :KV_WIDTH].astype(F32), bd_ref[:KV_WIDTH, :KV_WIDTH], gk_ref[...])
    if rope:
        q = _rope(q, cos_ref[...], sin_ref[...])
        k = _rope(k, cos_ref[...], sin_ref[...])
    q = q * (HEAD_DIM ** -0.5)
    for h in range(ATTN_HEADS):
        qo_ref[0, h] = q[:, h * HEAD_DIM:(h + 1) * HEAD_DIM].astype(BF16)
    v = kv[:, KV_WIDTH:].astype(F32)
    lane = lax.broadcasted_iota(I32, v.shape, 1)
    ones_col = jnp.where(lane == HEAD_DIM, 1.0, 0.0)
    for g in range(KV_HEADS):
        ko_ref[0, g] = k[:, g * HEAD_DIM:(g + 1) * HEAD_DIM].astype(BF16)
        vg = v if g == 0 else pltpu.roll(v, LANES - g * HEAD_DIM, 1)
        vo_ref[0, g] = jnp.where(lane < HEAD_DIM, vg, ones_col).astype(BF16)


def _qkprep_call(main, cos, sin, gq, gk, bd, tm, rope):
    b, t, _ = main.shape
    return pl.pallas_call(
        functools.partial(_qkprep_kernel, rope=rope),
        grid=(b, t // tm),
        in_specs=[pl.BlockSpec((1, tm, ATTN_WIDTH), lambda bi, i: (bi, i, COL_Q // ATTN_WIDTH)),
                  pl.BlockSpec((1, tm, 2 * KV_WIDTH), lambda bi, i: (bi, i, COL_KV // (2 * KV_WIDTH))),
                  pl.BlockSpec((tm, LANES), lambda bi, i: (i, 0)),
                  pl.BlockSpec((tm, LANES), lambda bi, i: (i, 0)),
                  pl.BlockSpec((1, ATTN_WIDTH), lambda bi, i: (0, 0)),
                  pl.BlockSpec((1, KV_WIDTH), lambda bi, i: (0, 0)),
                  pl.BlockSpec((ATTN_WIDTH, ATTN_WIDTH), lambda bi, i: (0, 0))],
        out_specs=[pl.BlockSpec((1, ATTN_HEADS, tm, HEAD_DIM), lambda bi, i: (bi, 0, i, 0)),
                   pl.BlockSpec((1, KV_HEADS, tm, HEAD_DIM), lambda bi, i: (bi, 0, i, 0)),
                   pl.BlockSpec((1, KV_HEADS, tm, LANES), lambda bi, i: (bi, 0, i, 0))],
        out_shape=[jax.ShapeDtypeStruct((b, ATTN_HEADS, t, HEAD_DIM), BF16),
                   jax.ShapeDtypeStruct((b, KV_HEADS, t, HEAD_DIM), BF16),
                   jax.ShapeDtypeStruct((b, KV_HEADS, t, LANES), BF16)],
        compiler_params=_cparams("parallel", "parallel"),
        name="qkprep",
    )(main, main, cos, sin, gq, gk, bd)


ATTN_BOUND_LIMIT = 40.0


def _attn_kernel(shift_ref, q_ref, *refs, bounded):
    o_ref = refs[-1]
    sources = list(zip(refs[:-1:2], refs[1:-1:2]))
    outs = []
    for r in range(ATTN_REP):
        q = q_ref[0, r]
        scores = [lax.dot_general(q, k_ref[0, 0], _NT, preferred_element_type=F32) for k_ref, _ in sources]
        if bounded:
            shift = shift_ref[0]
        else:
            shift = functools.reduce(jnp.maximum, [jnp.max(s, axis=-1, keepdims=True) for s in scores])
        acc = sum(_dot(jnp.exp(s - shift).astype(BF16), v_ref[0, 0]) for s, (_, v_ref) in zip(scores, sources))
        outs.append(acc[:, :HEAD_DIM] / acc[:, HEAD_DIM:HEAD_DIM + 1])
    o_ref[0] = jnp.concatenate(outs, axis=1).astype(o_ref.dtype)


def _attn_call(qh, kv_sources, bound, tq):
    b, _, t, _ = qh.shape
    in_specs = [pl.BlockSpec(memory_space=pltpu.SMEM),
                pl.BlockSpec((1, ATTN_REP, tq, HEAD_DIM), lambda bi, g, i: (bi, g, i, 0))]
    args = [bound.reshape(1), qh]
    for kh, vh in kv_sources:
        tk = kh.shape[2]
        in_specs += [pl.BlockSpec((1, 1, tk, HEAD_DIM), lambda bi, g, i: (bi, g, 0, 0)),
                     pl.BlockSpec((1, 1, tk, LANES), lambda bi, g, i: (bi, g, 0, 0))]
        args += [kh, vh]

    def call(bounded):
        return pl.pallas_call(
            functools.partial(_attn_kernel, bounded=bounded),
            grid=(b, KV_HEADS, t // tq),
            in_specs=in_specs,
            out_specs=pl.BlockSpec((1, tq, ATTN_REP * HEAD_DIM), lambda bi, g, i: (bi, i, g)),
            out_shape=jax.ShapeDtypeStruct((b, t, ATTN_WIDTH), BF16),
            compiler_params=_cparams("parallel", "parallel", "parallel"),
            name="attn_bounded" if bounded else "attn_rowmax",
        )(*args)

    return lax.cond(bound <= ATTN_BOUND_LIMIT, lambda: call(True), lambda: call(False))


def _conv3(u, prev_row, next_row, w):
    tm = u.shape[0]
    rid = lax.broadcasted_iota(I32, u.shape, 0)
    up = jnp.where(rid == 0, prev_row, pltpu.roll(u, 1, 0))
    dn = jnp.where(rid == tm - 1, next_row, pltpu.roll(u, tm - 1, 0))
    return up * w[0:1, :] + u * w[1:2, :] + dn * w[2:3, :]


def _conv_kernel(x_ref, xp_ref, xn_ref, c_ref, cp_ref, cn_ref, s_ref, sp_ref, sn_ref, b_ref,
                 w_ref, bias_ref, wsc_ref, xo_ref, yo_ref):
    i, n = pl.program_id(1), pl.num_programs(1)
    keep_p = jnp.where(i > 0, 1.0, 0.0)
    keep_n = jnp.where(i < n - 1, 1.0, 0.0)
    last = BF16_ROWS - 1
    f = lambda r: r.astype(F32)
    u = f(x_ref[0])
    y = _conv3(u, f(xp_ref[0, last:last + 1, :]) * keep_p, f(xn_ref[0, 0:1, :]) * keep_n, w_ref[...]) + bias_ref[...]
    xo_ref[0] = _silu(y).astype(xo_ref.dtype)
    cs = f(c_ref[0]) * f(s_ref[0])
    cs_p = f(cp_ref[0, last:last + 1, :]) * f(sp_ref[0, last:last + 1, :]) * keep_p
    cs_n = f(cn_ref[0, 0:1, :]) * f(sn_ref[0, 0:1, :]) * keep_n
    yo_ref[0] = (f(b_ref[0]) * _conv3(cs, cs_p, cs_n, wsc_ref[...])).astype(yo_ref.dtype)


def _conv_call(main, conv_w, conv_b, sc_w, tm):
    b, t, _ = main.shape
    hb = tm // BF16_ROWS
    nh = t // BF16_ROWS

    def centre(width, col):
        return pl.BlockSpec((1, tm, width), lambda bi, i: (bi, i, col // width))

    def prev(width, col):
        return pl.BlockSpec((1, BF16_ROWS, width), lambda bi, i: (bi, jnp.maximum(i * hb - 1, 0), col // width))

    def nxt(width, col):
        return pl.BlockSpec((1, BF16_ROWS, width), lambda bi, i: (bi, jnp.minimum((i + 1) * hb, nh - 1), col // width))

    const = lambda shape: pl.BlockSpec(shape, lambda bi, i: (0, 0))
    return pl.pallas_call(
        _conv_kernel,
        grid=(b, t // tm),
        in_specs=[centre(SSD_CONV_DIM, COL_XBC), prev(SSD_CONV_DIM, COL_XBC), nxt(SSD_CONV_DIM, COL_XBC),
                  centre(SC_WIDTH, COL_SCC), prev(SC_WIDTH, COL_SCC), nxt(SC_WIDTH, COL_SCC),
                  centre(SC_WIDTH, COL_SCX), prev(SC_WIDTH, COL_SCX), nxt(SC_WIDTH, COL_SCX),
                  centre(SC_WIDTH, COL_SCB),
                  const((3, SSD_CONV_DIM)), const((1, SSD_CONV_DIM)), const((3, SC_WIDTH))],
        out_specs=[pl.BlockSpec((1, tm, SSD_CONV_DIM), lambda bi, i: (bi, i, 0)),
                   pl.BlockSpec((1, tm, SC_WIDTH), lambda bi, i: (bi, i, 0))],
        out_shape=[jax.ShapeDtypeStruct((b, t, SSD_CONV_DIM), BF16),
                   jax.ShapeDtypeStruct((b, t, SC_WIDTH), BF16)],
        compiler_params=_cparams("parallel", "parallel"),
        name="conv",
    )(main, main, main, main, main, main, main, main, main, main, conv_w, conv_b, sc_w)


def _ssd_kernel(xa_ref, dt_ref, dtT_ref, z_ref, s0_ref, brow_ref, arow_ref, bcol_ref, acol_ref, dexp_ref, nw_ref,
                y_ref, so_ref, yf_ref, st_ref):
    L = SSD_CHUNK
    t = xa_ref.shape[1]
    nc = t // L
    ri = lax.broadcasted_iota(I32, (L, L), 0)
    ci = lax.broadcasted_iota(I32, (L, L), 1)
    lane512 = lax.broadcasted_iota(I32, (L, SSD_INNER), 1)
    row512 = lax.broadcasted_iota(I32, (L, SSD_INNER), 0)
    blockmask = jnp.where((row512 >> 6) == (lane512 >> 8), 1.0, 0.0)
    lane128 = lax.broadcasted_iota(I32, (L, LANES), 1)
    a_row = -jnp.exp(arow_ref[...])
    a_col = -jnp.exp(acol_ref[...])

    st_ref[...] = s0_ref[0]

    def chunk(c, d):
        r0 = pl.multiple_of(c * L, L)
        causal = (ri >= ci) if d == 0 else (ri <= ci)
        tri = jnp.where(causal, 1.0, 0.0).astype(BF16)
        triT = jnp.where((ri <= ci) if d == 0 else (ri >= ci), 1.0, 0.0).astype(BF16)
        expand = jnp.where(row512 == (lane512 >> 6) + d * SSD_HEADS, 1.0, 0.0).astype(BF16)
        xa = xa_ref[0, pl.ds(r0, L), :]
        x = xa[:, :SSD_INNER]
        b_pair = xa[:, SSD_INNER:SSD_INNER + SSD_GN]
        c_pair = xa[:, SSD_INNER + SSD_GN:]
        dt_all = _softplus(dt_ref[0, pl.ds(r0, L), :] + brow_ref[...])
        a1, a2, a3 = _split3(dt_all * a_row)
        acs = _dot(tri, a1) + _dot(tri, a2) + _dot(tri, a3)
        dtT = _softplus(dtT_ref[0, :, pl.ds(r0, L)] + bcol_ref[...])
        acsT = _dot_exact_lhs(dtT * a_col, triT)
        edge = acs[L - 1:L, :] if d == 0 else acs[0:1, :]
        w_all = jnp.exp(edge - acs) * dt_all
        e_exp = jnp.exp(_dot_exact_lhs(acs, expand))
        w_exp = _dot_exact_lhs(w_all, expand)
        state = st_ref[d]
        y_off = _dot(c_pair, state.astype(BF16)) * e_exp
        gmats = [lax.dot_general(jnp.where((lane128 >> 6) == g, c_pair, jnp.zeros_like(c_pair)), b_pair, _NT,
                                 preferred_element_type=F32) for g in range(SSD_GROUPS)]
        pairs = []
        for pr in range(SSD_HEADS // 2):
            x_pair = x[:, pr * LANES:(pr + 1) * LANES]
            acc = jnp.zeros((L, LANES), F32)
            for hh in range(2):
                h = 2 * pr + hh
                gmat = gmats[h // (SSD_HEADS // SSD_GROUPS)]
                k = d * SSD_HEADS + h
                seg = acs[:, k:k + 1] - acsT[k:k + 1, :]
                m = jnp.where(causal, jnp.exp(jnp.where(causal, seg, 0.0)), 0.0) * gmat * dtT[k:k + 1, :]
                xh = jnp.where((lane128 >> 6) == hh, x_pair, jnp.zeros_like(x_pair))
                acc = acc + _dot(m.astype(BF16), xh)
            pairs.append(acc)
        y = y_off + jnp.concatenate(pairs, axis=1)
        e_edge = e_exp[L - 1:L, :] if d == 0 else e_exp[0:1, :]
        ds = lax.dot_general(b_pair, (x.astype(F32) * w_exp).astype(BF16), _TN, preferred_element_type=F32)
        st_ref[d] = (state * e_edge + ds) * blockmask
        return y, x, r0

    def fwd(c, carry):
        y, _, r0 = chunk(c, 0)
        yf_ref[pl.ds(r0, L), :] = y
        return carry

    unroll = 2 if nc % 2 == 0 else 1
    lax.fori_loop(0, nc, fwd, 0, unroll=unroll)

    def bwd(i, carry):
        c = nc - 1 - i
        yb, x, r0 = chunk(c, 1)
        y = yf_ref[pl.ds(r0, L), :] + yb + dexp_ref[...] * x.astype(F32)
        gated = y * _silu(z_ref[0, pl.ds(r0, L), :].astype(F32))
        ms = jnp.mean(gated * gated, axis=-1, keepdims=True)
        y_ref[0, pl.ds(r0, L), :] = (gated * lax.rsqrt(ms + EPS) * nw_ref[...]).astype(y_ref.dtype)
        return carry

    lax.fori_loop(0, nc, bwd, 0, unroll=unroll)
    so_ref[0] = st_ref[...]


def _ssd_call(xa, dt, dtT, main, s0, brow, arow, bcol, acol, dexp, nw):
    b, t, _ = xa.shape
    c2 = lambda shape: pl.BlockSpec(shape, lambda bi: (0, 0))
    return pl.pallas_call(
        _ssd_kernel,
        grid=(b,),
        in_specs=[pl.BlockSpec((1, t, SSD_CONV_DIM), lambda bi: (bi, 0, 0)),
                  pl.BlockSpec((1, t, LANES), lambda bi: (bi, 0, 0)),
                  pl.BlockSpec((1, 2 * SSD_HEADS, t), lambda bi: (bi, 0, 0)),
                  pl.BlockSpec((1, t, SSD_INNER), lambda bi: (bi, 0, COL_Z // SSD_INNER)),
                  pl.BlockSpec((1, 2, SSD_GN, SSD_INNER), lambda bi: (bi, 0, 0, 0)),
                  c2((1, LANES)), c2((1, LANES)), c2((2 * SSD_HEADS, LANES)), c2((2 * SSD_HEADS, LANES)),
                  c2((1, SSD_INNER)), c2((1, SSD_INNER))],
        out_specs=[pl.BlockSpec((1, t, SSD_INNER), lambda bi: (bi, 0, 0)),
                   pl.BlockSpec((1, 2, SSD_GN, SSD_INNER), lambda bi: (bi, 0, 0, 0))],
        out_shape=[jax.ShapeDtypeStruct((b, t, SSD_INNER), BF16),
                   jax.ShapeDtypeStruct((b, 2, SSD_GN, SSD_INNER), F32)],
        scratch_shapes=[pltpu.VMEM((t, SSD_INNER), F32), pltpu.VMEM((2, SSD_GN, SSD_INNER), F32)],
        compiler_params=_cparams("parallel"),
        name="ssd",
    )(xa, dt, dtT, main, s0, brow, arow, bcol, acol, dexp, nw)


def _merge_kernel(ya_ref, ys_ref, yc_ref, g_ref, x_ref, m2_ref, m3_ref, m4_ref, wa_ref, ws_ref, wc_ref, wo_ref,
                  lg_ref, lb_ref, wr_ref, x1_ref, h2_ref, aff_ref, *, alpha):
    d = x_ref.shape[2]
    gate = lambda j: jax.nn.sigmoid(g_ref[0, :, j * d:(j + 1) * d].astype(F32))
    m = (gate(0) * _dot(ya_ref[0], wa_ref[...]) + gate(1) * _dot(ys_ref[0], ws_ref[...])
         + gate(2) * _dot(yc_ref[0], wc_ref[...]))
    out = _dot(m.astype(BF16), wo_ref[...])
    x1 = _layer_norm(alpha * x_ref[0] + m2_ref[0] * out) * lg_ref[...] + lb_ref[...]
    x1_ref[0] = x1
    h2 = _layer_norm(x1) * (1.0 + m4_ref[0]) + m3_ref[0]
    h2_ref[0] = h2.astype(h2_ref.dtype)
    logits = _dot_hp_nt(wr_ref[...], h2)
    e = jnp.exp(logits - jnp.max(logits, axis=0, keepdims=True))
    aff_ref[0] = e / jnp.sum(e, axis=0, keepdims=True)


def _merge_call(ya, ys, yc, main, x, m2, m3, m4, wa, ws, wc, wo, lg, lb, wrT, alpha, tm):
    b, t, d = x.shape
    tok = lambda w: pl.BlockSpec((1, tm, w), lambda bi, i: (bi, i, 0))
    per_b = pl.BlockSpec((1, 1, d), lambda bi, i: (bi, 0, 0))
    const = lambda shape: pl.BlockSpec(shape, lambda bi, i: (0, 0))
    return pl.pallas_call(
        functools.partial(_merge_kernel, alpha=alpha),
        grid=(b, t // tm),
        in_specs=[tok(ATTN_WIDTH), tok(SSD_INNER), tok(SC_WIDTH), tok(N_BRANCH * d), tok(d),
                  per_b, per_b, per_b,
                  const((ATTN_WIDTH, d)), const((SSD_INNER, d)), const((SC_WIDTH, d)), const((d, d)),
                  const((1, d)), const((1, d)), const((N_EXPERTS, d))],
        out_specs=[tok(d), tok(d), pl.BlockSpec((1, N_EXPERTS, tm), lambda bi, i: (bi, 0, i))],
        out_shape=[jax.ShapeDtypeStruct((b, t, d), F32), jax.ShapeDtypeStruct((b, t, d), BF16),
                   jax.ShapeDtypeStruct((b, N_EXPERTS, t), F32)],
        compiler_params=_cparams("parallel", "parallel"),
        name="merge",
    )(ya, ys, yc, main, x, m2, m3, m4, wa, ws, wc, wo, lg, lb, wrT)


def _route_kernel(aff_ref, pos_ref, w_ref, offs_ref, *, cap):
    aff = aff_ref[0]
    ne, t = aff.shape
    capf = float(cap)

    def step(i, lo):
        cand = lo | jnp.left_shift(jnp.int32(1), 30 - i)
        cnt = jnp.sum(jnp.where(aff >= pltpu.bitcast(cand, F32), 1.0, 0.0), axis=1, keepdims=True)
        return jnp.where(cnt >= capf, cand, lo)

    thr = pltpu.bitcast(lax.fori_loop(0, 31, step, jnp.zeros((ne, 1), I32)), F32)
    need = capf - jnp.sum(jnp.where(aff > thr, 1.0, 0.0), axis=1, keepdims=True)
    ri = lax.broadcasted_iota(I32, (LANES, LANES), 0)
    ci = lax.broadcasted_iota(I32, (LANES, LANES), 1)
    upper = jnp.where(ri < ci, 1.0, 0.0).astype(BF16)
    lane = lax.broadcasted_iota(I32, (ne, LANES), 1)
    carry_gt = jnp.zeros((ne, 1), F32)
    carry_eq = jnp.zeros((ne, 1), F32)
    offs = jnp.zeros((ne, LANES), F32)
    nblk = t // LANES
    for j in range(nblk):
        sl = slice(j * LANES, (j + 1) * LANES)
        aj = aff[:, sl]
        gj = jnp.where(aj > thr, 1.0, 0.0)
        ej = jnp.where(aj == thr, 1.0, 0.0)
        pre = _dot(jnp.concatenate([gj, ej], axis=0).astype(BF16), upper)
        pre_gt = pre[:ne] + carry_gt
        pre_eq = pre[ne:] + carry_eq
        sel = gj + ej * jnp.where(pre_eq < need, 1.0, 0.0)
        slot = pre_gt + jnp.minimum(pre_eq, need)
        pos_ref[0, :, sl] = jnp.where(sel > 0.5, slot, -1.0)
        w_ref[0, :, sl] = jnp.where(sel > 0.5, aj, 0.0)
        offs = jnp.where(lane == j, carry_gt + jnp.minimum(carry_eq, need), offs)
        carry_gt = carry_gt + jnp.sum(gj, axis=1, keepdims=True)
        carry_eq = carry_eq + jnp.sum(ej, axis=1, keepdims=True)
    offs = jnp.where(lane >= nblk, capf, offs)
    offs_ref[0] = offs.astype(I32)


def _route_call(affT, cap):
    b, ne, t = affT.shape
    spec = pl.BlockSpec((1, ne, t), lambda bi: (bi, 0, 0))
    return pl.pallas_call(
        functools.partial(_route_kernel, cap=cap),
        grid=(b,),
        in_specs=[spec],
        out_specs=[spec, spec, pl.BlockSpec((1, ne, LANES), lambda bi: (bi, 0, 0))],
        out_shape=[jax.ShapeDtypeStruct((b, ne, t), F32), jax.ShapeDtypeStruct((b, ne, t), F32),
                   jax.ShapeDtypeStruct((b, ne, LANES), I32)],
        compiler_params=_cparams("parallel"),
        name="route",
    )(affT)


EXPERT_GROUP = 4


def _slot_windows(offs_ref, bi, i, ne, per, cap, win):
    out = []
    for e in range(ne):
        lo = offs_ref[(bi * ne + e) * LANES + i * per]
        hi = offs_ref[(bi * ne + e) * LANES + (i + 1) * per]
        out.append((lo, hi, lo - lax.rem(lo, BF16_ROWS)))
    return out


def _window_start(wb, p, cap, win):
    return pl.multiple_of(jnp.minimum(wb + p * win, cap - win), BF16_ROWS)


def _gather_kernel(offs_ref, h_ref, pos_ref, *rest, win):
    xs_ref = rest[-1]
    bi, i = pl.program_id(0), pl.program_id(1)
    tm = h_ref.shape[1]
    ne, cap = xs_ref.shape[1], xs_ref.shape[2]
    windows = _slot_windows(offs_ref, bi, i, ne, tm // LANES, cap, win)
    h = h_ref[0]
    pos = pos_ref[0]
    row = lax.broadcasted_iota(I32, (win, tm), 0)

    @pl.when(i == 0)
    def _():
        xs_ref[...] = jnp.zeros_like(xs_ref)

    def onehot(e, start, first=None):
        slot = row + start
        if first is not None:
            slot = jnp.where(slot >= first, slot, -2)
        return jnp.where(pos[e:e + 1, :] == slot.astype(F32), 1.0, 0.0).astype(BF16)

    def add_window(e, start, rows):
        xs_ref[0, e, pl.ds(start, win), :] += rows.astype(xs_ref.dtype)

    for g0 in range(0, ne, EXPERT_GROUP):
        experts = range(g0, min(g0 + EXPERT_GROUP, ne))
        starts = [_window_start(windows[e][2], 0, cap, win) for e in experts]
        y = _dot(jnp.concatenate([onehot(e, s) for e, s in zip(experts, starts)], axis=0), h)
        for k, (e, s) in enumerate(zip(experts, starts)):
            add_window(e, s, y[k * win:(k + 1) * win])
    for p in range(1, cap // win):
        for e in range(ne):
            lo, hi, wb = windows[e]

            @pl.when(hi > wb + p * win)
            def _():
                s = _window_start(wb, p, cap, win)
                add_window(e, s, _dot(onehot(e, s, first=wb + p * win), h))


def _gather_call(offs, h2, pos, cap, tm, slots_total, slot_base, buf=None):
    b, t, d = h2.shape
    ne = pos.shape[1]
    in_specs = [pl.BlockSpec((1, tm, d), lambda bi, i, offs: (bi, i, 0)),
                pl.BlockSpec((1, ne, tm), lambda bi, i, offs: (bi, 0, i))]
    args = [offs, h2, pos]
    if buf is not None:
        in_specs.append(pl.BlockSpec(memory_space=pl.ANY))
        args.append(buf)
    return pl.pallas_call(
        functools.partial(_gather_kernel, win=min(LANES, cap)),
        grid_spec=pltpu.PrefetchScalarGridSpec(
            num_scalar_prefetch=1,
            grid=(b, t // tm),
            in_specs=in_specs,
            out_specs=pl.BlockSpec((1, ne, cap, d), lambda bi, i, offs: (bi, 0, slot_base // cap, 0))),
        out_shape=jax.ShapeDtypeStruct((b, ne, slots_total, d), BF16),
        input_output_aliases={} if buf is None else {3: 0},
        compiler_params=_cparams("parallel", "arbitrary"),
        name="gather",
    )(*args)


def _ffn_kernel(xs_ref, wg_ref, wu_ref, wd_ref, o_ref, acc_ref, *, group):
    f, nf = pl.program_id(1), pl.num_programs(1)
    b, _, slots, d = xs_ref.shape
    wg = wg_ref[0, 0].astype(BF16)
    wu = wu_ref[0, 0].astype(BF16)
    wd = wd_ref[0, 0].astype(BF16)

    @pl.when(f == 0)
    def _():
        acc_ref[...] = jnp.zeros_like(acc_ref)

    for b0 in range(0, b, group):
        x = xs_ref[b0:b0 + group, 0].reshape(group * slots, d)
        hid = (_silu(_dot(x, wg)) * _dot(x, wu)).astype(BF16)
        acc_ref[b0:b0 + group] += _dot(hid, wd).reshape(group, slots, d)

    @pl.when(f == nf - 1)
    def _():
        o_ref[:, 0] = acc_ref[...].astype(o_ref.dtype)


def _ffn_call(xs, wg, wu, wd, layer, tf):
    b, ne, slots, d = xs.shape
    ff = wg.shape[3]
    tok = pl.BlockSpec((b, 1, slots, d), lambda e, f: (0, e, 0, 0))
    return pl.pallas_call(
        functools.partial(_ffn_kernel, group=b // 2 if b % 2 == 0 else 1),
        grid=(ne, ff // tf),
        in_specs=[tok,
                  pl.BlockSpec((1, 1, d, tf), lambda e, f: (layer, e, 0, f)),
                  pl.BlockSpec((1, 1, d, tf), lambda e, f: (layer, e, 0, f)),
                  pl.BlockSpec((1, 1, tf, d), lambda e, f: (layer, e, f, 0))],
        out_specs=tok,
        out_shape=jax.ShapeDtypeStruct(xs.shape, BF16),
        scratch_shapes=[pltpu.VMEM((b, slots, d), F32)],
        compiler_params=_cparams("parallel", "arbitrary"),
        name="ffn",
    )(xs, wg, wu, wd)


def _combine_kernel(offs_ref, o_ref, pos_ref, w_ref, x_ref, m5_ref, lg_ref, lb_ref, y_ref, acc_ref, *, alpha, win):
    bi, i = pl.program_id(0), pl.program_id(1)
    tm = x_ref.shape[1]
    ne, cap = o_ref.shape[1], o_ref.shape[2]
    windows = _slot_windows(offs_ref, bi, i, ne, tm // LANES, cap, win)
    pos = pos_ref[0]
    w = w_ref[0]
    col = lax.broadcasted_iota(I32, (tm, win), 1)

    def onehot(e, start, first=None):
        slot = col + start
        if first is not None:
            slot = jnp.where(slot >= first, slot, -2)
        return jnp.where(pos[:, e:e + 1] == slot.astype(F32), w[:, e:e + 1], 0.0).astype(BF16)

    y = jnp.zeros((tm, o_ref.shape[3]), F32)
    for g0 in range(0, ne, EXPERT_GROUP):
        experts = range(g0, min(g0 + EXPERT_GROUP, ne))
        starts = [_window_start(windows[e][2], 0, cap, win) for e in experts]
        lhs = jnp.concatenate([onehot(e, s) for e, s in zip(experts, starts)], axis=1)
        rhs = jnp.concatenate([o_ref[0, e, pl.ds(s, win), :] for e, s in zip(experts, starts)], axis=0)
        y = y + _dot(lhs, rhs)
    acc_ref[...] = y
    for p in range(1, cap // win):
        for e in range(ne):
            lo, hi, wb = windows[e]

            @pl.when(hi > wb + p * win)
            def _():
                s = _window_start(wb, p, cap, win)
                acc_ref[...] += _dot(onehot(e, s, first=wb + p * win), o_ref[0, e, pl.ds(s, win), :])
    y_ref[0] = _layer_norm(alpha * x_ref[0] + m5_ref[0] * acc_ref[...]) * lg_ref[...] + lb_ref[...]


def _combine_call(offs, out, posT, wT, x1, m5, lg, lb, alpha, tm, cap, slot_base):
    b, t, d = x1.shape
    ne = out.shape[1]
    return pl.pallas_call(
        functools.partial(_combine_kernel, alpha=alpha, win=min(LANES, cap)),
        grid_spec=pltpu.PrefetchScalarGridSpec(
            num_scalar_prefetch=1,
            grid=(b, t // tm),
            in_specs=[pl.BlockSpec((1, ne, cap, d), lambda bi, i, offs: (bi, 0, slot_base // cap, 0)),
                      pl.BlockSpec((1, tm, ne), lambda bi, i, offs: (bi, i, 0)),
                      pl.BlockSpec((1, tm, ne), lambda bi, i, offs: (bi, i, 0)),
                      pl.BlockSpec((1, tm, d), lambda bi, i, offs: (bi, i, 0)),
                      pl.BlockSpec((1, 1, d), lambda bi, i, offs: (bi, 0, 0)),
                      pl.BlockSpec((1, d), lambda bi, i, offs: (0, 0)),
                      pl.BlockSpec((1, d), lambda bi, i, offs: (0, 0))],
            out_specs=pl.BlockSpec((1, tm, d), lambda bi, i, offs: (bi, i, 0)),
            scratch_shapes=[pltpu.VMEM((tm, d), F32)]),
        out_shape=jax.ShapeDtypeStruct((b, t, d), F32),
        compiler_params=_cparams("parallel", "arbitrary"),
        name="combine",
    )(offs, out, posT, wT, x1, m5, lg, lb)


def _rope_tables(n):
    rows = n // GRID_W
    row = jnp.repeat(jnp.arange(rows), GRID_W).astype(F32)
    col = jnp.tile(jnp.arange(GRID_W), rows).astype(F32)
    inv = ROPE_THETA ** (-jnp.arange(0, AXIS_ROT, 2, dtype=F32) / AXIS_ROT)
    ang = jnp.stack([row[:, None] * inv, col[:, None] * inv], axis=1)
    cos = jnp.repeat(jnp.cos(ang)[:, :, None, :], 2, axis=2).reshape(n, HEAD_DIM)
    sin = jnp.stack([-jnp.sin(ang), jnp.sin(ang)], axis=2).reshape(n, HEAD_DIM)
    return jnp.tile(cos, (1, LANES // HEAD_DIM)), jnp.tile(sin, (1, LANES // HEAD_DIM))


def _tile(t, pref):
    return pref if t % pref == 0 else t


def _moe(streams, wg, wu, wd, layer, lg, lb, alpha):
    caps = [CAPACITY_FACTOR * h2.shape[1] // N_EXPERTS for _, h2, _, _ in streams]
    bases = [sum(caps[:k]) for k in range(len(caps))]
    assert all(base % cap == 0 for base, cap in zip(bases, caps)) and sum(caps) % BF16_ROWS == 0
    xs, routed = None, []
    for (_, h2, affT, _), cap, base in zip(streams, caps, bases):
        t = h2.shape[1]
        pos, wsel, offs = _route_call(affT, cap)
        offs = offs.reshape(-1)
        xs = _gather_call(offs, h2, pos, cap, tm=_tile(t, 512), slots_total=sum(caps), slot_base=base, buf=xs)
        routed.append((offs, jnp.swapaxes(pos, 1, 2), jnp.swapaxes(wsel, 1, 2)))
    out = _ffn_call(xs, wg, wu, wd, layer, tf=_tile(wg.shape[3], 512))
    return [_combine_call(offs, out, posT, wT, x1, m5, lg, lb, alpha, tm=_tile(x1.shape[1], 512),
                          cap=cap, slot_base=base)
            for (x1, _, _, m5), (offs, posT, wT), cap, base in zip(streams, routed, caps, bases)]


def kernel(x, c, ctx, c_ctx, w_mod, b_mod, w_in, q_norm, k_norm, ssd_conv_w, ssd_conv_b, ssd_a_log, ssd_dt_bias,
           ssd_d, ssd_norm, sc_conv_w, w_br_attn, w_br_ssd, w_br_conv, w_o, ln1_g, ln1_b, w_router, w_exp_gate,
           w_exp_up, w_exp_down, ln2_g, ln2_b):
    b, n, d = x.shape
    nctx = ctx.shape[1]
    depth = w_mod.shape[0]
    alpha = (2 * depth) ** 0.25

    rows = -(-(b + 1) // 8) * 8
    cvec = jnp.concatenate([c, c_ctx[None], jnp.zeros((rows - b - 1, d), F32)], axis=0)
    mods = _mod_call(cvec, w_mod, b_mod)

    cos_l, sin_l = _rope_tables(n)
    cos_c, sin_c = jnp.ones((nctx, LANES), F32), jnp.zeros((nctx, LANES), F32)
    hid = jnp.arange(ATTN_WIDTH) // HEAD_DIM
    bd = (hid[:, None] == hid[None, :]).astype(BF16)
    dexp_all = jnp.repeat(ssd_d, SSD_HEAD_DIM, axis=1)
    pad16 = lambda v: jnp.pad(v.reshape(1, 2 * SSD_HEADS), ((0, 0), (0, LANES - 2 * SSD_HEADS)))
    col16 = lambda v: jnp.broadcast_to(v.reshape(2 * SSD_HEADS, 1), (2 * SSD_HEADS, LANES))

    x_ctx = ctx
    for i in range(depth):
        need_ctx = i < depth - 1
        wi = w_in[i]
        k_, v_, xbc_, dt_, q_, z_, scb_, scc_, scx_, gate_ = jnp.split(
            wi, [128, 256, 1024, 1040, 1552, 2064, 2576, 3088, 3600], axis=1)
        w_main = jnp.concatenate([gate_, xbc_, k_, v_, q_, z_, scb_, scc_, scx_], axis=1).astype(BF16)
        w_dt = jnp.pad(dt_, ((0, 0), (0, LANES - 2 * SSD_HEADS)))
        w_dtT = dt_.T
        gq = jnp.tile(q_norm[i], ATTN_HEADS)[None]
        gk = jnp.tile(k_norm[i], KV_HEADS)[None]
        brow, arow = pad16(ssd_dt_bias[i]), pad16(ssd_a_log[i])
        bcol, acol = col16(ssd_dt_bias[i]), col16(ssd_a_log[i])
        dexp, nw = dexp_all[i][None], ssd_norm[i][None]
        wa, ws, wc, wo = (w.astype(BF16) for w in (w_br_attn[i], w_br_ssd[i], w_br_conv[i], w_o[i]))
        wrT = w_router[i].T
        lg1, lb1, lg2, lb2 = ln1_g[i][None], ln1_b[i][None], ln2_g[i][None], ln2_b[i][None]
        mod_l = [m[:, None, :] for m in jnp.split(mods[i, :b], N_MOD, axis=-1)]
        mod_c = [jnp.broadcast_to(m[None, None, :], (b, 1, d)) for m in jnp.split(mods[i, b], N_MOD, axis=-1)]

        def mixer(xin, mod, cos, sin, rope, s0):
            t = xin.shape[1]
            main, dt, dtT = _inproj_call(xin, mod[0], mod[1], w_main, w_dt, w_dtT, tm=_tile(t, 512))
            qh, kh, vh = _qkprep_call(main, cos, sin, gq, gk, bd, tm=_tile(t, 512), rope=rope)
            xa, yc = _conv_call(main, ssd_conv_w[i], ssd_conv_b[i][None], sc_conv_w[i], tm=_tile(t, 512))
            ys, s_out = _ssd_call(xa, dt, dtT, main, s0, brow, arow, bcol, acol, dexp, nw)
            return main, qh, kh, vh, yc, ys, s_out

        def merge(xin, mod, main, ya, ys, yc):
            return _merge_call(ya, ys, yc, main, xin, mod[2], mod[3], mod[4], wa, ws, wc, wo, lg1, lb1, wrT,
                               alpha, tm=_tile(xin.shape[1], 512))

        zero_state = jnp.zeros((b, 2, SSD_GN, SSD_INNER), F32)
        main_c, qh_c, kh_c, vh_c, yc_c, ys_c, s_ctx = mixer(x_ctx, mod_c, cos_c, sin_c, False, zero_state)
        main_l, qh_l, kh_l, vh_l, yc_l, ys_l, _ = mixer(x, mod_l, cos_l, sin_l, True, s_ctx)
        bound = math.sqrt(HEAD_DIM) * jnp.max(jnp.abs(q_norm[i])) * jnp.max(jnp.abs(k_norm[i]))
        ya_l = _attn_call(qh_l, [(kh_c, vh_c), (kh_l, vh_l)], bound, tq=_tile(n, 256))
        streams = [(*merge(x, mod_l, main_l, ya_l, ys_l, yc_l), mod_l[5])]
        if need_ctx:
            ya_c = _attn_call(qh_c, [(kh_c, vh_c)], bound, tq=_tile(nctx, 256))
            streams.append((*merge(x_ctx, mod_c, main_c, ya_c, ys_c, yc_c), mod_c[5]))
        new = _moe(streams, w_exp_gate, w_exp_up, w_exp_down, i, lg2, lb2, alpha)
        x = new[0]
        if need_ctx:
            x_ctx = new[1]
    return x
```

```python
import functools
import math

import jax
import jax.numpy as jnp
from jax import lax
from jax.experimental import pallas as pl
from jax.experimental.pallas import tpu as pltpu

F32, BF16, I32 = jnp.float32, jnp.bfloat16, jnp.int32

HEAD_DIM = 64
ATTN_HEADS = 8
KV_HEADS = 2
ATTN_REP = ATTN_HEADS // KV_HEADS
ATTN_WIDTH = ATTN_HEADS * HEAD_DIM
KV_WIDTH = KV_HEADS * HEAD_DIM
AXIS_ROT = HEAD_DIM // 2
ROPE_THETA = 10000.0
GRID_W = 64
SSD_HEADS = 8
SSD_HEAD_DIM = 64
SSD_INNER = SSD_HEADS * SSD_HEAD_DIM
SSD_STATE = 64
SSD_GROUPS = 2
SSD_GN = SSD_GROUPS * SSD_STATE
SSD_CONV_DIM = SSD_INNER + 2 * SSD_GN
SSD_CHUNK = 128
SC_WIDTH = 512
N_BRANCH = 3
N_EXPERTS = 16
CAPACITY_FACTOR = 2
N_MOD = 6
EPS = 1e-6

LANES = 128
BF16_ROWS = 16
VMEM_LIMIT_BYTES = 56 * 1024 * 1024

COL_GATE, COL_XBC, COL_KV, COL_Q, COL_Z, COL_SCB, COL_SCC, COL_SCX = 0, 3072, 3840, 4096, 4608, 5120, 5632, 6144
MAIN_COLS = 6656

_NT = (((1,), (1,)), ((), ()))
_TN = (((0,), (0,)), ((), ()))


def _cparams(*sem):
    return pltpu.CompilerParams(dimension_semantics=sem, vmem_limit_bytes=VMEM_LIMIT_BYTES)


def _dot(a, b):
    return jnp.dot(a, b, preferred_element_type=F32)


def _split2(a):
    hi = a.astype(BF16)
    lo = (a - hi.astype(F32)).astype(BF16)
    return hi, lo


def _split3(a):
    p1 = a.astype(BF16)
    r1 = a - p1.astype(F32)
    p2 = r1.astype(BF16)
    p3 = (r1 - p2.astype(F32)).astype(BF16)
    return p1, p2, p3


def _dot_hp(a, w):
    ah, al = _split2(a)
    wh, wl = _split2(w)
    return _dot(ah, wh) + _dot(al, wh) + _dot(ah, wl)


def _dot_hp_nt(a, w):
    ah, al = _split2(a)
    wh, wl = _split2(w)
    d = lambda x, y: lax.dot_general(x, y, _NT, preferred_element_type=F32)
    return d(ah, wh) + d(al, wh) + d(ah, wl)


def _dot_exact_lhs(a, m):
    p1, p2, p3 = _split3(a)
    return _dot(p1, m) + _dot(p2, m) + _dot(p3, m)


def _layer_norm(x):
    mu = jnp.mean(x, axis=-1, keepdims=True)
    xc = x - mu
    var = jnp.mean(xc * xc, axis=-1, keepdims=True)
    return xc * lax.rsqrt(var + EPS)


def _softplus(x):
    return jnp.maximum(x, 0.0) + jnp.log(1.0 + jnp.exp(-jnp.abs(x)))


def _silu(x):
    return x * jax.nn.sigmoid(x)


def _mod_kernel(c_ref, w_ref, b_ref, o_ref):
    o_ref[0] = _dot_hp(_silu(c_ref[...]), w_ref[0]) + b_ref[0]


def _mod_call(cvec, w_mod, b_mod):
    depth, d, n = w_mod.shape
    rows = cvec.shape[0]
    tn = 1536
    return pl.pallas_call(
        _mod_kernel,
        grid=(depth, n // tn),
        in_specs=[pl.BlockSpec((rows, d), lambda l, j: (0, 0)),
                  pl.BlockSpec((1, d, tn), lambda l, j: (l, 0, j)),
                  pl.BlockSpec((1, 1, tn), lambda l, j: (l, 0, j))],
        out_specs=pl.BlockSpec((1, rows, tn), lambda l, j: (l, 0, j)),
        out_shape=jax.ShapeDtypeStruct((depth, rows, n), F32),
        compiler_params=_cparams("parallel", "parallel"),
        name="mod",
    )(cvec, w_mod, b_mod.reshape(depth, 1, n))


_REF_COLS = dict(k=(0, 128), v=(128, 256), xbc=(256, 1024), dt=(1024, 1040), q=(1040, 1552), z=(1552, 2064),
                 scb=(2064, 2576), scc=(2576, 3088), scx=(3088, 3600), gate=(3600, 6672))
_MAIN_ORDER = ("gate", "xbc", "k", "v", "q", "z", "scb", "scc", "scx")


def _wprep_kernel(w_ref, o_ref):
    w = w_ref[0]
    o_ref[0] = jnp.concatenate([w[:, _REF_COLS[n][0]:_REF_COLS[n][1]] for n in _MAIN_ORDER], axis=1).astype(o_ref.dtype)


def _wprep_call(w_in, tr=128):
    depth, d, cols = w_in.shape
    return pl.pallas_call(
        _wprep_kernel,
        grid=(depth, d // tr),
        in_specs=[pl.BlockSpec((1, tr, cols), lambda l, r: (l, r, 0))],
        out_specs=pl.BlockSpec((1, tr, MAIN_COLS), lambda l, r: (l, r, 0)),
        out_shape=jax.ShapeDtypeStruct((depth, d, MAIN_COLS), BF16),
        compiler_params=_cparams("parallel", "parallel"),
        name="wprep",
    )(w_in)


def _inproj_kernel(x_ref, sh_ref, sc_ref, w_ref, wdt_ref, o_ref, dt_ref, dtT_ref):
    h = _layer_norm(x_ref[0]) * (1.0 + sc_ref[0]) + sh_ref[0]
    hb = h.astype(BF16)
    step = 512
    for c0 in range(0, MAIN_COLS, step):
        o_ref[0, :, c0:c0 + step] = _dot(hb, w_ref[0, :, c0:c0 + step]).astype(o_ref.dtype)
    dt = _dot_hp(h, wdt_ref[...])
    dt_ref[0] = dt
    dtT_ref[0] = dt.T[:2 * SSD_HEADS, :]


def _inproj_call(x, shift, scale, w_main, layer, w_dt, tm):
    b, t, d = x.shape
    return pl.pallas_call(
        _inproj_kernel,
        grid=(b, t // tm),
        in_specs=[pl.BlockSpec((1, tm, d), lambda bi, i: (bi, i, 0)),
                  pl.BlockSpec((1, 1, d), lambda bi, i: (bi, 0, 0)),
                  pl.BlockSpec((1, 1, d), lambda bi, i: (bi, 0, 0)),
                  pl.BlockSpec((1, d, MAIN_COLS), lambda bi, i: (layer, 0, 0)),
                  pl.BlockSpec((d, LANES), lambda bi, i: (0, 0))],
        out_specs=[pl.BlockSpec((1, tm, MAIN_COLS), lambda bi, i: (bi, i, 0)),
                   pl.BlockSpec((1, tm, LANES), lambda bi, i: (bi, i, 0)),
                   pl.BlockSpec((1, 2 * SSD_HEADS, tm), lambda bi, i: (bi, 0, i))],
        out_shape=[jax.ShapeDtypeStruct((b, t, MAIN_COLS), BF16),
                   jax.ShapeDtypeStruct((b, t, LANES), F32),
                   jax.ShapeDtypeStruct((b, 2 * SSD_HEADS, t), F32)],
        compiler_params=_cparams("parallel", "parallel"),
        name="inproj",
    )(x, shift, scale, w_main, w_dt)


def _head_rms(x, bd, g):
    hi, lo = _split2(x * x)
    ss = _dot(hi, bd) + _dot(lo, bd)
    return x * lax.rsqrt(ss * (1.0 / HEAD_DIM) + EPS) * g


def _rope(x, cos, sin_signed):
    half = AXIS_ROT // 2
    outs = []
    for j in range(x.shape[1] // LANES):
        xj = x[:, j * LANES:(j + 1) * LANES]
        lane = lax.broadcasted_iota(I32, xj.shape, 1)
        partner = jnp.where((lane & half) == 0, pltpu.roll(xj, LANES - half, 1), pltpu.roll(xj, half, 1))
        outs.append(xj * cos + partner * sin_signed)
    return outs[0] if len(outs) == 1 else jnp.concatenate(outs, axis=1)


def _qkprep_kernel(q_ref, kv_ref, cos_ref, sin_ref, gq_ref, gk_ref, bd_ref, qo_ref, ko_ref, vo_ref, *, rope):
    q = _head_rms(q_ref[0].astype(F32), bd_ref[...], gq_ref[...])
    kv = kv_ref[0]
    k = _head_rms(kv[:, :KV_WIDTH].astype(F32), bd_ref[:KV_WIDTH, :KV_WIDTH], gk_ref[...])
    if rope:
        q = _rope(q, cos_ref[...], sin_ref[...])
        k = _rope(k, cos_ref[...], sin_ref[...])
    q = q * (HEAD_DIM ** -0.5)
    for h in range(ATTN_HEADS):
        qo_ref[0, h] = q[:, h * HEAD_DIM:(h + 1) * HEAD_DIM].astype(BF16)
    v = kv[:, KV_WIDTH:].astype(F32)
    lane = lax.broadcasted_iota(I32, v.shape, 1)
    ones_col = jnp.where(lane == HEAD_DIM, 1.0, 0.0)
    for g in range(KV_HEADS):
        ko_ref[0, g] = k[:, g * HEAD_DIM:(g + 1) * HEAD_DIM].astype(BF16)
        vg = v if g == 0 else pltpu.roll(v, LANES - g * HEAD_DIM, 1)
        vo_ref[0, g] = jnp.where(lane < HEAD_DIM, vg, ones_col).astype(BF16)


def _qkprep_call(main, cos, sin, gq, gk, bd, tm, rope):
    b, t, _ = main.shape
    return pl.pallas_call(
        functools.partial(_qkprep_kernel, rope=rope),
        grid=(b, t // tm),
        in_specs=[pl.BlockSpec((1, tm, ATTN_WIDTH), lambda bi, i: (bi, i, COL_Q // ATTN_WIDTH)),
                  pl.BlockSpec((1, tm, 2 * KV_WIDTH), lambda bi, i: (bi, i, COL_KV // (2 * KV_WIDTH))),
                  pl.BlockSpec((tm, LANES), lambda bi, i: (i, 0)),
                  pl.BlockSpec((tm, LANES), lambda bi, i: (i, 0)),
                  pl.BlockSpec((1, ATTN_WIDTH), lambda bi, i: (0, 0)),
                  pl.BlockSpec((1, KV_WIDTH), lambda bi, i: (0, 0)),
                  pl.BlockSpec((ATTN_WIDTH, ATTN_WIDTH), lambda bi, i: (0, 0))],
        out_specs=[pl.BlockSpec((1, ATTN_HEADS, tm, HEAD_DIM), lambda bi, i: (bi, 0, i, 0)),
                   pl.BlockSpec((1, KV_HEADS, tm, HEAD_DIM), lambda bi, i: (bi, 0, i, 0)),
                   pl.BlockSpec((1, KV_HEADS, tm, LANES), lambda bi, i: (bi, 0, i, 0))],
        out_shape=[jax.ShapeDtypeStruct((b, ATTN_HEADS, t, HEAD_DIM), BF16),
                   jax.ShapeDtypeStruct((b, KV_HEADS, t, HEAD_DIM), BF16),
                   jax.ShapeDtypeStruct((b, KV_HEADS, t, LANES), BF16)],
        compiler_params=_cparams("parallel", "parallel"),
        name="qkprep",
    )(main, main, cos, sin, gq, gk, bd)


ATTN_BOUND_LIMIT = 40.0


def _attn_kernel(shift_ref, q_ref, *refs, bounded):
    o_ref = refs[-1]
    sources = list(zip(refs[:-1:2], refs[1:-1:2]))
    outs = []
    for r in range(ATTN_REP):
        q = q_ref[0, r]
        scores = [lax.dot_general(q, k_ref[0, 0], _NT, preferred_element_type=F32) for k_ref, _ in sources]
        if bounded:
            shift = shift_ref[0]
        else:
            shift = functools.reduce(jnp.maximum, [jnp.max(s, axis=-1, keepdims=True) for s in scores])
        acc = sum(_dot(jnp.exp(s - shift).astype(BF16), v_ref[0, 0]) for s, (_, v_ref) in zip(scores, sources))
        outs.append(acc[:, :HEAD_DIM] / acc[:, HEAD_DIM:HEAD_DIM + 1])
    o_ref[0] = jnp.concatenate(outs, axis=1).astype(o_ref.dtype)


def _attn_call(qh, kv_sources, bound, tq):
    b, _, t, _ = qh.shape
    in_specs = [pl.BlockSpec(memory_space=pltpu.SMEM),
                pl.BlockSpec((1, ATTN_REP, tq, HEAD_DIM), lambda bi, g, i: (bi, g, i, 0))]
    args = [bound.reshape(1), qh]
    for kh, vh in kv_sources:
        tk = kh.shape[2]
        in_specs += [pl.BlockSpec((1, 1, tk, HEAD_DIM), lambda bi, g, i: (bi, g, 0, 0)),
                     pl.BlockSpec((1, 1, tk, LANES), lambda bi, g, i: (bi, g, 0, 0))]
        args += [kh, vh]

    def call(bounded):
        return pl.pallas_call(
            functools.partial(_attn_kernel, bounded=bounded),
            grid=(b, KV_HEADS, t // tq),
            in_specs=in_specs,
            out_specs=pl.BlockSpec((1, tq, ATTN_REP * HEAD_DIM), lambda bi, g, i: (bi, i, g)),
            out_shape=jax.ShapeDtypeStruct((b, t, ATTN_WIDTH), BF16),
            compiler_params=_cparams("parallel", "parallel", "parallel"),
            name="attn_bounded" if bounded else "attn_rowmax",
        )(*args)

    return lax.cond(bound <= ATTN_BOUND_LIMIT, lambda: call(True), lambda: call(False))


def _conv3(u, prev_row, next_row, w):
    tm = u.shape[0]
    rid = lax.broadcasted_iota(I32, u.shape, 0)
    up = jnp.where(rid == 0, prev_row, pltpu.roll(u, 1, 0))
    dn = jnp.where(rid == tm - 1, next_row, pltpu.roll(u, tm - 1, 0))
    return up * w[0:1, :] + u * w[1:2, :] + dn * w[2:3, :]


def _conv_kernel(x_ref, xp_ref, xn_ref, c_ref, cp_ref, cn_ref, s_ref, sp_ref, sn_ref, b_ref,
                 w_ref, bias_ref, wsc_ref, xo_ref, yo_ref):
    i, n = pl.program_id(1), pl.num_programs(1)
    keep_p = jnp.where(i > 0, 1.0, 0.0)
    keep_n = jnp.where(i < n - 1, 1.0, 0.0)
    last = BF16_ROWS - 1
    f = lambda r: r.astype(F32)
    u = f(x_ref[0])
    y = _conv3(u, f(xp_ref[0, last:last + 1, :]) * keep_p, f(xn_ref[0, 0:1, :]) * keep_n, w_ref[...]) + bias_ref[...]
    xo_ref[0] = _silu(y).astype(xo_ref.dtype)
    cs = f(c_ref[0]) * f(s_ref[0])
    cs_p = f(cp_ref[0, last:last + 1, :]) * f(sp_ref[0, last:last + 1, :]) * keep_p
    cs_n = f(cn_ref[0, 0:1, :]) * f(sn_ref[0, 0:1, :]) * keep_n
    yo_ref[0] = (f(b_ref[0]) * _conv3(cs, cs_p, cs_n, wsc_ref[...])).astype(yo_ref.dtype)


def _conv_call(main, conv_w, conv_b, sc_w, tm):
    b, t, _ = main.shape
    hb = tm // BF16_ROWS
    nh = t // BF16_ROWS

    def centre(width, col):
        return pl.BlockSpec((1, tm, width), lambda bi, i: (bi, i, col // width))

    def prev(width, col):
        return pl.BlockSpec((1, BF16_ROWS, width), lambda bi, i: (bi, jnp.maximum(i * hb - 1, 0), col // width))

    def nxt(width, col):
        return pl.BlockSpec((1, BF16_ROWS, width), lambda bi, i: (bi, jnp.minimum((i + 1) * hb, nh - 1), col // width))

    const = lambda shape: pl.BlockSpec(shape, lambda bi, i: (0, 0))
    return pl.pallas_call(
        _conv_kernel,
        grid=(b, t // tm),
        in_specs=[centre(SSD_CONV_DIM, COL_XBC), prev(SSD_CONV_DIM, COL_XBC), nxt(SSD_CONV_DIM, COL_XBC),
                  centre(SC_WIDTH, COL_SCC), prev(SC_WIDTH, COL_SCC), nxt(SC_WIDTH, COL_SCC),
                  centre(SC_WIDTH, COL_SCX), prev(SC_WIDTH, COL_SCX), nxt(SC_WIDTH, COL_SCX),
                  centre(SC_WIDTH, COL_SCB),
                  const((3, SSD_CONV_DIM)), const((1, SSD_CONV_DIM)), const((3, SC_WIDTH))],
        out_specs=[pl.BlockSpec((1, tm, SSD_CONV_DIM), lambda bi, i: (bi, i, 0)),
                   pl.BlockSpec((1, tm, SC_WIDTH), lambda bi, i: (bi, i, 0))],
        out_shape=[jax.ShapeDtypeStruct((b, t, SSD_CONV_DIM), BF16),
                   jax.ShapeDtypeStruct((b, t, SC_WIDTH), BF16)],
        compiler_params=_cparams("parallel", "parallel"),
        name="conv",
    )(main, main, main, main, main, main, main, main, main, main, conv_w, conv_b, sc_w)


def _ssd_kernel(xa_ref, dt_ref, dtT_ref, z_ref, s0_ref, brow_ref, arow_ref, bcol_ref, acol_ref, dexp_ref, nw_ref,
                y_ref, so_ref, yf_ref, st_ref):
    L = SSD_CHUNK
    t = xa_ref.shape[1]
    nc = t // L
    ri = lax.broadcasted_iota(I32, (L, L), 0)
    ci = lax.broadcasted_iota(I32, (L, L), 1)
    lane512 = lax.broadcasted_iota(I32, (L, SSD_INNER), 1)
    row512 = lax.broadcasted_iota(I32, (L, SSD_INNER), 0)
    blockmask = jnp.where((row512 >> 6) == (lane512 >> 8), 1.0, 0.0)
    lane128 = lax.broadcasted_iota(I32, (L, LANES), 1)
    a_row = -jnp.exp(arow_ref[...])
    a_col = -jnp.exp(acol_ref[...])

    st_ref[...] = s0_ref[0]

    def chunk(c, d):
        r0 = pl.multiple_of(c * L, L)
        causal = (ri >= ci) if d == 0 else (ri <= ci)
        tri = jnp.where(causal, 1.0, 0.0).astype(BF16)
        triT = jnp.where((ri <= ci) if d == 0 else (ri >= ci), 1.0, 0.0).astype(BF16)
        expand = jnp.where(row512 == (lane512 >> 6) + d * SSD_HEADS, 1.0, 0.0).astype(BF16)
        xa = xa_ref[0, pl.ds(r0, L), :]
        x = xa[:, :SSD_INNER]
        b_pair = xa[:, SSD_INNER:SSD_INNER + SSD_GN]
        c_pair = xa[:, SSD_INNER + SSD_GN:]
        dt_all = _softplus(dt_ref[0, pl.ds(r0, L), :] + brow_ref[...])
        a1, a2, a3 = _split3(dt_all * a_row)
        acs = _dot(tri, a1) + _dot(tri, a2) + _dot(tri, a3)
        dtT = _softplus(dtT_ref[0, :, pl.ds(r0, L)] + bcol_ref[...])
        acsT = _dot_exact_lhs(dtT * a_col, triT)
        edge = acs[L - 1:L, :] if d == 0 else acs[0:1, :]
        w_all = jnp.exp(edge - acs) * dt_all
        e_exp = jnp.exp(_dot_exact_lhs(acs, expand))
        w_exp = _dot(w_all.astype(BF16), expand)
        state = st_ref[d]
        y_off = _dot(c_pair, state.astype(BF16)) * e_exp
        gmats = [lax.dot_general(jnp.where((lane128 >> 6) == g, c_pair, jnp.zeros_like(c_pair)), b_pair, _NT,
                                 preferred_element_type=F32) for g in range(SSD_GROUPS)]
        pairs = []
        for pr in range(SSD_HEADS // 2):
            x_pair = x[:, pr * LANES:(pr + 1) * LANES]
            acc = jnp.zeros((L, LANES), F32)
            for hh in range(2):
                h = 2 * pr + hh
                gmat = gmats[h // (SSD_HEADS // SSD_GROUPS)]
                k = d * SSD_HEADS + h
                seg = acs[:, k:k + 1] - acsT[k:k + 1, :]
                m = jnp.where(causal, jnp.exp(jnp.where(causal, seg, 0.0)), 0.0) * gmat * dtT[k:k + 1, :]
                xh = jnp.where((lane128 >> 6) == hh, x_pair, jnp.zeros_like(x_pair))
                acc = acc + _dot(m.astype(BF16), xh)
            pairs.append(acc)
        y = y_off + jnp.concatenate(pairs, axis=1)
        e_edge = e_exp[L - 1:L, :] if d == 0 else e_exp[0:1, :]
        ds = lax.dot_general(b_pair, (x.astype(F32) * w_exp).astype(BF16), _TN, preferred_element_type=F32)
        st_ref[d] = (state * e_edge + ds) * blockmask
        return y, x, r0

    def fwd(c, carry):
        y, _, r0 = chunk(c, 0)
        yf_ref[pl.ds(r0, L), :] = y
        return carry

    unroll = 4 if nc % 4 == 0 else (2 if nc % 2 == 0 else 1)
    lax.fori_loop(0, nc, fwd, 0, unroll=unroll)

    def bwd(i, carry):
        c = nc - 1 - i
        yb, x, r0 = chunk(c, 1)
        y = yf_ref[pl.ds(r0, L), :] + yb + dexp_ref[...] * x.astype(F32)
        gated = y * _silu(z_ref[0, pl.ds(r0, L), :].astype(F32))
        ms = jnp.mean(gated * gated, axis=-1, keepdims=True)
        y_ref[0, pl.ds(r0, L), :] = (gated * lax.rsqrt(ms + EPS) * nw_ref[...]).astype(y_ref.dtype)
        return carry

    lax.fori_loop(0, nc, bwd, 0, unroll=unroll)
    so_ref[0] = st_ref[...]


def _ssd_call(xa, dt, dtT, main, s0, brow, arow, bcol, acol, dexp, nw):
    b, t, _ = xa.shape
    c2 = lambda shape: pl.BlockSpec(shape, lambda bi: (0, 0))
    return pl.pallas_call(
        _ssd_kernel,
        grid=(b,),
        in_specs=[pl.BlockSpec((1, t, SSD_CONV_DIM), lambda bi: (bi, 0, 0)),
                  pl.BlockSpec((1, t, LANES), lambda bi: (bi, 0, 0)),
                  pl.BlockSpec((1, 2 * SSD_HEADS, t), lambda bi: (bi, 0, 0)),
                  pl.BlockSpec((1, t, SSD_INNER), lambda bi: (bi, 0, COL_Z // SSD_INNER)),
                  pl.BlockSpec((1, 2, SSD_GN, SSD_INNER), lambda bi: (bi, 0, 0, 0)),
                  c2((1, LANES)), c2((1, LANES)), c2((2 * SSD_HEADS, LANES)), c2((2 * SSD_HEADS, LANES)),
                  c2((1, SSD_INNER)), c2((1, SSD_INNER))],
        out_specs=[pl.BlockSpec((1, t, SSD_INNER), lambda bi: (bi, 0, 0)),
                   pl.BlockSpec((1, 2, SSD_GN, SSD_INNER), lambda bi: (bi, 0, 0, 0))],
        out_shape=[jax.ShapeDtypeStruct((b, t, SSD_INNER), BF16),
                   jax.ShapeDtypeStruct((b, 2, SSD_GN, SSD_INNER), F32)],
        scratch_shapes=[pltpu.VMEM((t, SSD_INNER), F32), pltpu.VMEM((2, SSD_GN, SSD_INNER), F32)],
        compiler_params=_cparams("parallel"),
        name="ssd",
    )(xa, dt, dtT, main, s0, brow, arow, bcol, acol, dexp, nw)


def _merge_kernel(ya_ref, ys_ref, yc_ref, g_ref, x_ref, m2_ref, m3_ref, m4_ref, wa_ref, ws_ref, wc_ref, wo_ref,
                  lg_ref, lb_ref, wr_ref, x1_ref, h2_ref, aff_ref, *, alpha):
    d = x_ref.shape[2]
    gate = lambda j: jax.nn.sigmoid(g_ref[0, :, j * d:(j + 1) * d].astype(F32))
    m = (gate(0) * _dot(ya_ref[0], wa_ref[...]) + gate(1) * _dot(ys_ref[0], ws_ref[...])
         + gate(2) * _dot(yc_ref[0], wc_ref[...]))
    out = _dot(m.astype(BF16), wo_ref[...])
    x1 = _layer_norm(alpha * x_ref[0] + m2_ref[0] * out) * lg_ref[...] + lb_ref[...]
    x1_ref[0] = x1
    h2 = _layer_norm(x1) * (1.0 + m4_ref[0]) + m3_ref[0]
    h2_ref[0] = h2.astype(h2_ref.dtype)
    logits = _dot_hp_nt(wr_ref[...], h2)
    e = jnp.exp(logits - jnp.max(logits, axis=0, keepdims=True))
    aff_ref[0] = e / jnp.sum(e, axis=0, keepdims=True)


def _merge_call(ya, ys, yc, main, x, m2, m3, m4, wa, ws, wc, wo, lg, lb, wrT, alpha, tm):
    b, t, d = x.shape
    tok = lambda w: pl.BlockSpec((1, tm, w), lambda bi, i: (bi, i, 0))
    per_b = pl.BlockSpec((1, 1, d), lambda bi, i: (bi, 0, 0))
    const = lambda shape: pl.BlockSpec(shape, lambda bi, i: (0, 0))
    return pl.pallas_call(
        functools.partial(_merge_kernel, alpha=alpha),
        grid=(b, t // tm),
        in_specs=[tok(ATTN_WIDTH), tok(SSD_INNER), tok(SC_WIDTH), tok(N_BRANCH * d), tok(d),
                  per_b, per_b, per_b,
                  const((ATTN_WIDTH, d)), const((SSD_INNER, d)), const((SC_WIDTH, d)), const((d, d)),
                  const((1, d)), const((1, d)), const((N_EXPERTS, d))],
        out_specs=[tok(d), tok(d), pl.BlockSpec((1, N_EXPERTS, tm), lambda bi, i: (bi, 0, i))],
        out_shape=[jax.ShapeDtypeStruct((b, t, d), F32), jax.ShapeDtypeStruct((b, t, d), BF16),
                   jax.ShapeDtypeStruct((b, N_EXPERTS, t), F32)],
        compiler_params=_cparams("parallel", "parallel"),
        name="merge",
    )(ya, ys, yc, main, x, m2, m3, m4, wa, ws, wc, wo, lg, lb, wrT)


def _route_kernel(aff_ref, pos_ref, w_ref, offs_ref, *, cap):
    aff = aff_ref[0]
    ne, t = aff.shape
    capf = float(cap)

    def step(i, lo):
        cand = lo | jnp.left_shift(jnp.int32(1), 30 - i)
        cnt = jnp.sum(jnp.where(aff >= pltpu.bitcast(cand, F32), 1.0, 0.0), axis=1, keepdims=True)
        return jnp.where(cnt >= capf, cand, lo)

    thr = pltpu.bitcast(lax.fori_loop(0, 31, step, jnp.zeros((ne, 1), I32)), F32)
    need = capf - jnp.sum(jnp.where(aff > thr, 1.0, 0.0), axis=1, keepdims=True)
    ri = lax.broadcasted_iota(I32, (LANES, LANES), 0)
    ci = lax.broadcasted_iota(I32, (LANES, LANES), 1)
    upper = jnp.where(ri < ci, 1.0, 0.0).astype(BF16)
    lane = lax.broadcasted_iota(I32, (ne, LANES), 1)
    carry_gt = jnp.zeros((ne, 1), F32)
    carry_eq = jnp.zeros((ne, 1), F32)
    offs = jnp.zeros((ne, LANES), F32)
    nblk = t // LANES
    for j in range(nblk):
        sl = slice(j * LANES, (j + 1) * LANES)
        aj = aff[:, sl]
        gj = jnp.where(aj > thr, 1.0, 0.0)
        ej = jnp.where(aj == thr, 1.0, 0.0)
        pre = _dot(jnp.concatenate([gj, ej], axis=0).astype(BF16), upper)
        pre_gt = pre[:ne] + carry_gt
        pre_eq = pre[ne:] + carry_eq
        sel = gj + ej * jnp.where(pre_eq < need, 1.0, 0.0)
        slot = pre_gt + jnp.minimum(pre_eq, need)
        pos_ref[0, :, sl] = jnp.where(sel > 0.5, slot, -1.0)
        w_ref[0, :, sl] = jnp.where(sel > 0.5, aj, 0.0)
        offs = jnp.where(lane == j, carry_gt + jnp.minimum(carry_eq, need), offs)
        carry_gt = carry_gt + jnp.sum(gj, axis=1, keepdims=True)
        carry_eq = carry_eq + jnp.sum(ej, axis=1, keepdims=True)
    offs = jnp.where(lane >= nblk, capf, offs)
    offs_ref[0] = offs.astype(I32)


def _route_call(affT, cap):
    b, ne, t = affT.shape
    spec = pl.BlockSpec((1, ne, t), lambda bi: (bi, 0, 0))
    return pl.pallas_call(
        functools.partial(_route_kernel, cap=cap),
        grid=(b,),
        in_specs=[spec],
        out_specs=[spec, spec, pl.BlockSpec((1, ne, LANES), lambda bi: (bi, 0, 0))],
        out_shape=[jax.ShapeDtypeStruct((b, ne, t), F32), jax.ShapeDtypeStruct((b, ne, t), F32),
                   jax.ShapeDtypeStruct((b, ne, LANES), I32)],
        compiler_params=_cparams("parallel"),
        name="route",
    )(affT)


EXPERT_GROUP = 4


def _slot_windows(offs_ref, bi, i, ne, per, cap, win):
    out = []
    for e in range(ne):
        lo = offs_ref[(bi * ne + e) * LANES + i * per]
        hi = offs_ref[(bi * ne + e) * LANES + (i + 1) * per]
        out.append((lo, hi, lo - lax.rem(lo, BF16_ROWS)))
    return out


def _window_start(wb, p, cap, win):
    return pl.multiple_of(jnp.minimum(wb + p * win, cap - win), BF16_ROWS)


def _gather_kernel(offs_ref, h_ref, pos_ref, *rest, win):
    xs_ref = rest[-1]
    bi, i = pl.program_id(0), pl.program_id(1)
    tm = h_ref.shape[1]
    ne, cap = xs_ref.shape[1], xs_ref.shape[2]
    windows = _slot_windows(offs_ref, bi, i, ne, tm // LANES, cap, win)
    h = h_ref[0]
    pos = pos_ref[0]
    row = lax.broadcasted_iota(I32, (win, tm), 0)

    @pl.when(i == 0)
    def _():
        xs_ref[...] = jnp.zeros_like(xs_ref)

    def onehot(e, start, first=None):
        slot = row + start
        if first is not None:
            slot = jnp.where(slot >= first, slot, -2)
        return jnp.where(pos[e:e + 1, :] == slot.astype(F32), 1.0, 0.0).astype(BF16)

    def add_window(e, start, rows):
        xs_ref[0, e, pl.ds(start, win), :] += rows.astype(xs_ref.dtype)

    for g0 in range(0, ne, EXPERT_GROUP):
        experts = range(g0, min(g0 + EXPERT_GROUP, ne))
        starts = [_window_start(windows[e][2], 0, cap, win) for e in experts]
        y = _dot(jnp.concatenate([onehot(e, s) for e, s in zip(experts, starts)], axis=0), h)
        for k, (e, s) in enumerate(zip(experts, starts)):
            add_window(e, s, y[k * win:(k + 1) * win])
    for p in range(1, pl.cdiv(cap, win)):
        for e in range(ne):
            lo, hi, wb = windows[e]

            @pl.when(hi > wb + p * win)
            def _():
                s = _window_start(wb, p, cap, win)
                add_window(e, s, _dot(onehot(e, s, first=wb + p * win), h))


def _gather_call(offs, h2, pos, cap, tm, slots_total, slot_base, buf=None):
    b, t, d = h2.shape
    ne = pos.shape[1]
    mean = tm * cap // t
    win = min(LANES, cap, pl.cdiv(mean + mean // 4 + BF16_ROWS, BF16_ROWS) * BF16_ROWS)
    in_specs = [pl.BlockSpec((1, tm, d), lambda bi, i, offs: (bi, i, 0)),
                pl.BlockSpec((1, ne, tm), lambda bi, i, offs: (bi, 0, i))]
    args = [offs, h2, pos]
    if buf is not None:
        in_specs.append(pl.BlockSpec(memory_space=pl.ANY))
        args.append(buf)
    return pl.pallas_call(
        functools.partial(_gather_kernel, win=win),
        grid_spec=pltpu.PrefetchScalarGridSpec(
            num_scalar_prefetch=1,
            grid=(b, t // tm),
            in_specs=in_specs,
            out_specs=pl.BlockSpec((1, ne, cap, d), lambda bi, i, offs: (bi, 0, slot_base // cap, 0))),
        out_shape=jax.ShapeDtypeStruct((b, ne, slots_total, d), BF16),
        input_output_aliases={} if buf is None else {3: 0},
        compiler_params=_cparams("parallel", "arbitrary"),
        name="gather",
    )(*args)


def _ffn_kernel(xs_ref, wg_ref, wu_ref, wd_ref, o_ref, acc_ref, *, group):
    f, nf = pl.program_id(1), pl.num_programs(1)
    b, _, slots, d = xs_ref.shape
    wg = wg_ref[0, 0].astype(BF16)
    wu = wu_ref[0, 0].astype(BF16)
    wd = wd_ref[0, 0].astype(BF16)

    @pl.when(f == 0)
    def _():
        acc_ref[...] = jnp.zeros_like(acc_ref)

    for b0 in range(0, b, group):
        x = xs_ref[b0:b0 + group, 0].reshape(group * slots, d)
        hid = (_silu(_dot(x, wg)) * _dot(x, wu)).astype(BF16)
        acc_ref[b0:b0 + group] += _dot(hid, wd).reshape(group, slots, d)

    @pl.when(f == nf - 1)
    def _():
        o_ref[:, 0] = acc_ref[...].astype(o_ref.dtype)


def _ffn_call(xs, wg, wu, wd, layer, tf):
    b, ne, slots, d = xs.shape
    ff = wg.shape[3]
    tok = pl.BlockSpec((b, 1, slots, d), lambda e, f: (0, e, 0, 0))
    return pl.pallas_call(
        functools.partial(_ffn_kernel, group=1),
        grid=(ne, ff // tf),
        in_specs=[tok,
                  pl.BlockSpec((1, 1, d, tf), lambda e, f: (layer, e, 0, f)),
                  pl.BlockSpec((1, 1, d, tf), lambda e, f: (layer, e, 0, f)),
                  pl.BlockSpec((1, 1, tf, d), lambda e, f: (layer, e, f, 0))],
        out_specs=tok,
        out_shape=jax.ShapeDtypeStruct(xs.shape, BF16),
        scratch_shapes=[pltpu.VMEM((b, slots, d), F32)],
        compiler_params=_cparams("parallel", "arbitrary"),
        name="ffn",
    )(xs, wg, wu, wd)


def _combine_kernel(offs_ref, o_ref, pos_ref, w_ref, x_ref, m5_ref, lg_ref, lb_ref, y_ref, acc_ref, *, alpha, win):
    bi, i = pl.program_id(0), pl.program_id(1)
    tm = x_ref.shape[1]
    ne, cap = o_ref.shape[1], o_ref.shape[2]
    windows = _slot_windows(offs_ref, bi, i, ne, tm // LANES, cap, win)
    pos = pos_ref[0]
    w = w_ref[0]
    col = lax.broadcasted_iota(I32, (tm, win), 1)

    def onehot(e, start, first=None):
        slot = col + start
        if first is not None:
            slot = jnp.where(slot >= first, slot, -2)
        return jnp.where(pos[:, e:e + 1] == slot.astype(F32), w[:, e:e + 1], 0.0).astype(BF16)

    y = jnp.zeros((tm, o_ref.shape[3]), F32)
    for g0 in range(0, ne, EXPERT_GROUP):
        experts = range(g0, min(g0 + EXPERT_GROUP, ne))
        starts = [_window_start(windows[e][2], 0, cap, win) for e in experts]
        lhs = jnp.concatenate([onehot(e, s) for e, s in zip(experts, starts)], axis=1)
        rhs = jnp.concatenate([o_ref[0, e, pl.ds(s, win), :] for e, s in zip(experts, starts)], axis=0)
        y = y + _dot(lhs, rhs)
    acc_ref[...] = y
    for p in range(1, cap // win):
        for e in range(ne):
            lo, hi, wb = windows[e]

            @pl.when(hi > wb + p * win)
            def _():
                s = _window_start(wb, p, cap, win)
                acc_ref[...] += _dot(onehot(e, s, first=wb + p * win), o_ref[0, e, pl.ds(s, win), :])
    y_ref[0] = _layer_norm(alpha * x_ref[0] + m5_ref[0] * acc_ref[...]) * lg_ref[...] + lb_ref[...]


def _combine_call(offs, out, posT, wT, x1, m5, lg, lb, alpha, tm, cap, slot_base):
    b, t, d = x1.shape
    ne = out.shape[1]
    return pl.pallas_call(
        functools.partial(_combine_kernel, alpha=alpha, win=min(LANES, cap)),
        grid_spec=pltpu.PrefetchScalarGridSpec(
            num_scalar_prefetch=1,
            grid=(b, t // tm),
            in_specs=[pl.BlockSpec((1, ne, cap, d), lambda bi, i, offs: (bi, 0, slot_base // cap, 0)),
                      pl.BlockSpec((1, tm, ne), lambda bi, i, offs: (bi, i, 0)),
                      pl.BlockSpec((1, tm, ne), lambda bi, i, offs: (bi, i, 0)),
                      pl.BlockSpec((1, tm, d), lambda bi, i, offs: (bi, i, 0)),
                      pl.BlockSpec((1, 1, d), lambda bi, i, offs: (bi, 0, 0)),
                      pl.BlockSpec((1, d), lambda bi, i, offs: (0, 0)),
                      pl.BlockSpec((1, d), lambda bi, i, offs: (0, 0))],
            out_specs=pl.BlockSpec((1, tm, d), lambda bi, i, offs: (bi, i, 0)),
            scratch_shapes=[pltpu.VMEM((tm, d), F32)]),
        out_shape=jax.ShapeDtypeStruct((b, t, d), F32),
        compiler_params=_cparams("parallel", "arbitrary"),
        name="combine",
    )(offs, out, posT, wT, x1, m5, lg, lb)


def _rope_tables(n):
    rows = n // GRID_W
    row = jnp.repeat(jnp.arange(rows), GRID_W).astype(F32)
    col = jnp.tile(jnp.arange(GRID_W), rows).astype(F32)
    inv = ROPE_THETA ** (-jnp.arange(0, AXIS_ROT, 2, dtype=F32) / AXIS_ROT)
    ang = jnp.stack([row[:, None] * inv, col[:, None] * inv], axis=1)
    cos = jnp.repeat(jnp.cos(ang)[:, :, None, :], 2, axis=2).reshape(n, HEAD_DIM)
    sin = jnp.stack([-jnp.sin(ang), jnp.sin(ang)], axis=2).reshape(n, HEAD_DIM)
    return jnp.tile(cos, (1, LANES // HEAD_DIM)), jnp.tile(sin, (1, LANES // HEAD_DIM))


def _tile(t, pref):
    return pref if t % pref == 0 else t


def _moe(streams, wg, wu, wd, layer, lg, lb, alpha):
    caps = [CAPACITY_FACTOR * h2.shape[1] // N_EXPERTS for _, h2, _, _ in streams]
    bases = [sum(caps[:k]) for k in range(len(caps))]
    assert all(base % cap == 0 for base, cap in zip(bases, caps)) and sum(caps) % BF16_ROWS == 0
    xs, routed = None, []
    for (_, h2, affT, _), cap, base in zip(streams, caps, bases):
        t = h2.shape[1]
        pos, wsel, offs = _route_call(affT, cap)
        offs = offs.reshape(-1)
        xs = _gather_call(offs, h2, pos, cap, tm=_tile(t, 512), slots_total=sum(caps), slot_base=base, buf=xs)
        routed.append((offs, jnp.swapaxes(pos, 1, 2), jnp.swapaxes(wsel, 1, 2)))
    out = _ffn_call(xs, wg, wu, wd, layer, tf=_tile(wg.shape[3], 512))
    return [_combine_call(offs, out, posT, wT, x1, m5, lg, lb, alpha, tm=_tile(x1.shape[1], 512),
                          cap=cap, slot_base=base)
            for (x1, _, _, m5), (offs, posT, wT), cap, base in zip(streams, routed, caps, bases)]


def kernel(x, c, ctx, c_ctx, w_mod, b_mod, w_in, q_norm, k_norm, ssd_conv_w, ssd_conv_b, ssd_a_log, ssd_dt_bias,
           ssd_d, ssd_norm, sc_conv_w, w_br_attn, w_br_ssd, w_br_conv, w_o, ln1_g, ln1_b, w_router, w_exp_gate,
           w_exp_up, w_exp_down, ln2_g, ln2_b):
    b, n, d = x.shape
    nctx = ctx.shape[1]
    depth = w_mod.shape[0]
    alpha = (2 * depth) ** 0.25

    rows = -(-(b + 1) // 8) * 8
    cvec = jnp.concatenate([c, c_ctx[None], jnp.zeros((rows - b - 1, d), F32)], axis=0)
    mods = _mod_call(cvec, w_mod, b_mod)

    cos_l, sin_l = _rope_tables(n)
    cos_c, sin_c = jnp.ones((nctx, LANES), F32), jnp.zeros((nctx, LANES), F32)
    hid = jnp.arange(ATTN_WIDTH) // HEAD_DIM
    bd = (hid[:, None] == hid[None, :]).astype(BF16)
    dexp_all = jnp.repeat(ssd_d, SSD_HEAD_DIM, axis=1)
    pad16 = lambda v: jnp.pad(v.reshape(1, 2 * SSD_HEADS), ((0, 0), (0, LANES - 2 * SSD_HEADS)))
    col16 = lambda v: jnp.broadcast_to(v.reshape(2 * SSD_HEADS, 1), (2 * SSD_HEADS, LANES))

    w_main = _wprep_call(w_in)
    x_ctx = ctx
    for i in range(depth):
        need_ctx = i < depth - 1
        dt_lo, dt_hi = _REF_COLS["dt"]
        w_dt = jnp.pad(w_in[i, :, dt_lo:dt_hi], ((0, 0), (0, LANES - 2 * SSD_HEADS)))
        gq = jnp.tile(q_norm[i], ATTN_HEADS)[None]
        gk = jnp.tile(k_norm[i], KV_HEADS)[None]
        brow, arow = pad16(ssd_dt_bias[i]), pad16(ssd_a_log[i])
        bcol, acol = col16(ssd_dt_bias[i]), col16(ssd_a_log[i])
        dexp, nw = dexp_all[i][None], ssd_norm[i][None]
        wa, ws, wc, wo = (w.astype(BF16) for w in (w_br_attn[i], w_br_ssd[i], w_br_conv[i], w_o[i]))
        wrT = w_router[i].T
        lg1, lb1, lg2, lb2 = ln1_g[i][None], ln1_b[i][None], ln2_g[i][None], ln2_b[i][None]
        mod_l = [m[:, None, :] for m in jnp.split(mods[i, :b], N_MOD, axis=-1)]
        mod_c = [jnp.broadcast_to(m[None, None, :], (b, 1, d)) for m in jnp.split(mods[i, b], N_MOD, axis=-1)]

        def mixer(xin, mod, cos, sin, rope, s0):
            t = xin.shape[1]
            main, dt, dtT = _inproj_call(xin, mod[0], mod[1], w_main, i, w_dt, tm=_tile(t, 512))
            qh, kh, vh = _qkprep_call(main, cos, sin, gq, gk, bd, tm=_tile(t, 512), rope=rope)
            xa, yc = _conv_call(main, ssd_conv_w[i], ssd_conv_b[i][None], sc_conv_w[i], tm=_tile(t, 512))
            ys, s_out = _ssd_call(xa, dt, dtT, main, s0, brow, arow, bcol, acol, dexp, nw)
            return main, qh, kh, vh, yc, ys, s_out

        def merge(xin, mod, main, ya, ys, yc):
            return _merge_call(ya, ys, yc, main, xin, mod[2], mod[3], mod[4], wa, ws, wc, wo, lg1, lb1, wrT,
                               alpha, tm=_tile(xin.shape[1], 512))

        zero_state = jnp.zeros((b, 2, SSD_GN, SSD_INNER), F32)
        main_c, qh_c, kh_c, vh_c, yc_c, ys_c, s_ctx = mixer(x_ctx, mod_c, cos_c, sin_c, False, zero_state)
        main_l, qh_l, kh_l, vh_l, yc_l, ys_l, _ = mixer(x, mod_l, cos_l, sin_l, True, s_ctx)
        bound = math.sqrt(HEAD_DIM) * jnp.max(jnp.abs(q_norm[i])) * jnp.max(jnp.abs(k_norm[i]))
        ya_l = _attn_call(qh_l, [(kh_c, vh_c), (kh_l, vh_l)], bound, tq=_tile(n, 256))
        streams = [(*merge(x, mod_l, main_l, ya_l, ys_l, yc_l), mod_l[5])]
        if need_ctx:
            ya_c = _attn_call(qh_c, [(kh_c, vh_c)], bound, tq=_tile(nctx, 256))
            streams.append((*merge(x_ctx, mod_c, main_c, ya_c, ys_c, yc_c), mod_c[5]))
        new = _moe(streams, w_exp_gate, w_exp_up, w_exp_down, i, lg2, lb2, alpha)
        x = new[0]
        if need_ctx:
            x_ctx = new[1]
    return x
```

```python
import functools
import math

import jax
import jax.numpy as jnp
from jax import lax
from jax.experimental import pallas as pl
from jax.experimental.pallas import tpu as pltpu

F32, BF16, I32 = jnp.float32, jnp.bfloat16, jnp.int32

HEAD_DIM = 64
ATTN_HEADS = 8
KV_HEADS = 2
ATTN_REP = ATTN_HEADS // KV_HEADS
ATTN_WIDTH = ATTN_HEADS * HEAD_DIM
KV_WIDTH = KV_HEADS * HEAD_DIM
AXIS_ROT = HEAD_DIM // 2
ROPE_THETA = 10000.0
GRID_W = 64
SSD_HEADS = 8
SSD_HEAD_DIM = 64
SSD_INNER = SSD_HEADS * SSD_HEAD_DIM
SSD_STATE = 64
SSD_GROUPS = 2
SSD_GN = SSD_GROUPS * SSD_STATE
SSD_CONV_DIM = SSD_INNER + 2 * SSD_GN
SSD_CHUNK = 128
SC_WIDTH = 512
N_BRANCH = 3
N_EXPERTS = 16
CAPACITY_FACTOR = 2
N_MOD = 6
EPS = 1e-6

LANES = 128
BF16_ROWS = 16
VMEM_LIMIT_BYTES = 56 * 1024 * 1024

COL_GATE, COL_XBC, COL_KV, COL_Q, COL_Z, COL_SCB, COL_SCC, COL_SCX = 0, 3072, 3840, 4096, 4608, 5120, 5632, 6144
MAIN_COLS = 6656

_NT = (((1,), (1,)), ((), ()))
_TN = (((0,), (0,)), ((), ()))


def _cparams(*sem):
    return pltpu.CompilerParams(dimension_semantics=sem, vmem_limit_bytes=VMEM_LIMIT_BYTES)


def _dot(a, b):
    return jnp.dot(a, b, preferred_element_type=F32)


def _split2(a):
    hi = a.astype(BF16)
    lo = (a - hi.astype(F32)).astype(BF16)
    return hi, lo


def _split3(a):
    p1 = a.astype(BF16)
    r1 = a - p1.astype(F32)
    p2 = r1.astype(BF16)
    p3 = (r1 - p2.astype(F32)).astype(BF16)
    return p1, p2, p3


def _dot_hp(a, w):
    ah, al = _split2(a)
    wh, wl = _split2(w)
    return _dot(ah, wh) + _dot(al, wh) + _dot(ah, wl)


def _dot_hp_nt(a, w):
    ah, al = _split2(a)
    wh, wl = _split2(w)
    d = lambda x, y: lax.dot_general(x, y, _NT, preferred_element_type=F32)
    return d(ah, wh) + d(al, wh) + d(ah, wl)


def _dot_exact_lhs(a, m):
    p1, p2, p3 = _split3(a)
    return _dot(p1, m) + _dot(p2, m) + _dot(p3, m)


def _layer_norm(x):
    mu = jnp.mean(x, axis=-1, keepdims=True)
    xc = x - mu
    var = jnp.mean(xc * xc, axis=-1, keepdims=True)
    return xc * lax.rsqrt(var + EPS)


def _softplus(x):
    return jnp.maximum(x, 0.0) + jnp.log(1.0 + jnp.exp(-jnp.abs(x)))


def _silu(x):
    return x * jax.nn.sigmoid(x)


def _mod_kernel(c_ref, w_ref, b_ref, o_ref):
    o_ref[0] = _dot_hp(_silu(c_ref[...]), w_ref[0]) + b_ref[0]


def _mod_call(cvec, w_mod, b_mod):
    depth, d, n = w_mod.shape
    rows = cvec.shape[0]
    tn = 1536
    return pl.pallas_call(
        _mod_kernel,
        grid=(depth, n // tn),
        in_specs=[pl.BlockSpec((rows, d), lambda l, j: (0, 0)),
                  pl.BlockSpec((1, d, tn), lambda l, j: (l, 0, j)),
                  pl.BlockSpec((1, 1, tn), lambda l, j: (l, 0, j))],
        out_specs=pl.BlockSpec((1, rows, tn), lambda l, j: (l, 0, j)),
        out_shape=jax.ShapeDtypeStruct((depth, rows, n), F32),
        compiler_params=_cparams("parallel", "parallel"),
        name="mod",
    )(cvec, w_mod, b_mod.reshape(depth, 1, n))


_REF_COLS = dict(k=(0, 128), v=(128, 256), xbc=(256, 1024), dt=(1024, 1040), q=(1040, 1552), z=(1552, 2064),
                 scb=(2064, 2576), scc=(2576, 3088), scx=(3088, 3600), gate=(3600, 6672))
_MAIN_ORDER = ("gate", "xbc", "k", "v", "q", "z", "scb", "scc", "scx")


def _wprep_kernel(w_ref, o_ref, dt_ref):
    w = w_ref[0]
    o_ref[0] = jnp.concatenate([w[:, _REF_COLS[n][0]:_REF_COLS[n][1]] for n in _MAIN_ORDER], axis=1).astype(o_ref.dtype)
    dt_lo, dt_hi = _REF_COLS["dt"]
    blk = w[:, dt_lo:dt_lo + LANES]
    lane = lax.broadcasted_iota(I32, blk.shape, 1)
    dt_ref[0] = jnp.where(lane < dt_hi - dt_lo, blk, 0.0)


def _wprep_call(w_in, tr=128):
    depth, d, cols = w_in.shape
    return pl.pallas_call(
        _wprep_kernel,
        grid=(depth, d // tr),
        in_specs=[pl.BlockSpec((1, tr, cols), lambda l, r: (l, r, 0))],
        out_specs=[pl.BlockSpec((1, tr, MAIN_COLS), lambda l, r: (l, r, 0)),
                   pl.BlockSpec((1, tr, LANES), lambda l, r: (l, r, 0))],
        out_shape=[jax.ShapeDtypeStruct((depth, d, MAIN_COLS), BF16),
                   jax.ShapeDtypeStruct((depth, d, LANES), F32)],
        compiler_params=_cparams("parallel", "parallel"),
        name="wprep",
    )(w_in)


def _inproj_kernel(x_ref, sh_ref, sc_ref, w_ref, wdt_ref, o_ref, dt_ref, dtT_ref):
    h = _layer_norm(x_ref[0]) * (1.0 + sc_ref[0]) + sh_ref[0]
    hb = h.astype(BF16)
    step = 512
    for c0 in range(0, MAIN_COLS, step):
        o_ref[0, :, c0:c0 + step] = _dot(hb, w_ref[0, :, c0:c0 + step]).astype(o_ref.dtype)
    dt = _dot_hp(h, wdt_ref[0])
    dt_ref[0] = dt
    dtT_ref[0] = dt.T[:2 * SSD_HEADS, :]


def _inproj_call(x, shift, scale, w_main, layer, w_dt, tm):
    b, t, d = x.shape
    return pl.pallas_call(
        _inproj_kernel,
        grid=(b, t // tm),
        in_specs=[pl.BlockSpec((1, tm, d), lambda bi, i: (bi, i, 0)),
                  pl.BlockSpec((1, 1, d), lambda bi, i: (bi, 0, 0)),
                  pl.BlockSpec((1, 1, d), lambda bi, i: (bi, 0, 0)),
                  pl.BlockSpec((1, d, MAIN_COLS), lambda bi, i: (layer, 0, 0)),
                  pl.BlockSpec((1, d, LANES), lambda bi, i: (layer, 0, 0))],
        out_specs=[pl.BlockSpec((1, tm, MAIN_COLS), lambda bi, i: (bi, i, 0)),
                   pl.BlockSpec((1, tm, LANES), lambda bi, i: (bi, i, 0)),
                   pl.BlockSpec((1, 2 * SSD_HEADS, tm), lambda bi, i: (bi, 0, i))],
        out_shape=[jax.ShapeDtypeStruct((b, t, MAIN_COLS), BF16),
                   jax.ShapeDtypeStruct((b, t, LANES), F32),
                   jax.ShapeDtypeStruct((b, 2 * SSD_HEADS, t), F32)],
        compiler_params=_cparams("parallel", "parallel"),
        name="inproj",
    )(x, shift, scale, w_main, w_dt)


def _head_rms(x, bd, g):
    hi, lo = _split2(x * x)
    ss = _dot(hi, bd) + _dot(lo, bd)
    return x * lax.rsqrt(ss * (1.0 / HEAD_DIM) + EPS) * g


def _rope(x, cos, sin_signed):
    half = AXIS_ROT // 2
    outs = []
    for j in range(x.shape[1] // LANES):
        xj = x[:, j * LANES:(j + 1) * LANES]
        lane = lax.broadcasted_iota(I32, xj.shape, 1)
        partner = jnp.where((lane & half) == 0, pltpu.roll(xj, LANES - half, 1), pltpu.roll(xj, half, 1))
        outs.append(xj * cos + partner * sin_signed)
    return outs[0] if len(outs) == 1 else jnp.concatenate(outs, axis=1)


def _qkprep_kernel(q_ref, kv_ref, cos_ref, sin_ref, gq_ref, gk_ref, bd_ref, qo_ref, ko_ref, vo_ref, *, rope):
    q = _head_rms(q_ref[0].astype(F32), bd_ref[...], gq_ref[...])
    kv = kv_ref[0]
    k = _head_rms(kv[:, :KV_WIDTH].astype(F32), bd_ref[:KV_WIDTH, :KV_WIDTH], gk_ref[...])
    if rope:
        q = _rope(q, cos_ref[...], sin_ref[...])
        k = _rope(k, cos_ref[...], sin_ref[...])
    q = q * (HEAD_DIM ** -0.5)
    for h in range(ATTN_HEADS):
        qo_ref[0, h] = q[:, h * HEAD_DIM:(h + 1) * HEAD_DIM].astype(BF16)
    v = kv[:, KV_WIDTH:].astype(F32)
    lane = lax.broadcasted_iota(I32, v.shape, 1)
    ones_col = jnp.where(lane == HEAD_DIM, 1.0, 0.0)
    for g in range(KV_HEADS):
        ko_ref[0, g] = k[:, g * HEAD_DIM:(g + 1) * HEAD_DIM].astype(BF16)
        vg = v if g == 0 else pltpu.roll(v, LANES - g * HEAD_DIM, 1)
        vo_ref[0, g] = jnp.where(lane < HEAD_DIM, vg, ones_col).astype(BF16)


def _qkprep_call(main, cos, sin, gq, gk, bd, tm, rope):
    b, t, _ = main.shape
    return pl.pallas_call(
        functools.partial(_qkprep_kernel, rope=rope),
        grid=(b, t // tm),
        in_specs=[pl.BlockSpec((1, tm, ATTN_WIDTH), lambda bi, i: (bi, i, COL_Q // ATTN_WIDTH)),
                  pl.BlockSpec((1, tm, 2 * KV_WIDTH), lambda bi, i: (bi, i, COL_KV // (2 * KV_WIDTH))),
                  pl.BlockSpec((tm, LANES), lambda bi, i: (i, 0)),
                  pl.BlockSpec((tm, LANES), lambda bi, i: (i, 0)),
                  pl.BlockSpec((1, ATTN_WIDTH), lambda bi, i: (0, 0)),
                  pl.BlockSpec((1, KV_WIDTH), lambda bi, i: (0, 0)),
                  pl.BlockSpec((ATTN_WIDTH, ATTN_WIDTH), lambda bi, i: (0, 0))],
        out_specs=[pl.BlockSpec((1, ATTN_HEADS, tm, HEAD_DIM), lambda bi, i: (bi, 0, i, 0)),
                   pl.BlockSpec((1, KV_HEADS, tm, HEAD_DIM), lambda bi, i: (bi, 0, i, 0)),
                   pl.BlockSpec((1, KV_HEADS, tm, LANES), lambda bi, i: (bi, 0, i, 0))],
        out_shape=[jax.ShapeDtypeStruct((b, ATTN_HEADS, t, HEAD_DIM), BF16),
                   jax.ShapeDtypeStruct((b, KV_HEADS, t, HEAD_DIM), BF16),
                   jax.ShapeDtypeStruct((b, KV_HEADS, t, LANES), BF16)],
        compiler_params=_cparams("parallel", "parallel"),
        name="qkprep",
    )(main, main, cos, sin, gq, gk, bd)


ATTN_BOUND_LIMIT = 40.0


def _attn_kernel(shift_ref, q_ref, *refs, bounded):
    o_ref = refs[-1]
    sources = list(zip(refs[:-1:2], refs[1:-1:2]))
    outs = []
    for r in range(ATTN_REP):
        q = q_ref[0, r]
        scores = [lax.dot_general(q, k_ref[0, 0], _NT, preferred_element_type=F32) for k_ref, _ in sources]
        if bounded:
            shift = shift_ref[0]
        else:
            shift = functools.reduce(jnp.maximum, [jnp.max(s, axis=-1, keepdims=True) for s in scores])
        acc = sum(_dot(jnp.exp(s - shift).astype(BF16), v_ref[0, 0]) for s, (_, v_ref) in zip(scores, sources))
        outs.append(acc[:, :HEAD_DIM] / acc[:, HEAD_DIM:HEAD_DIM + 1])
    o_ref[0] = jnp.concatenate(outs, axis=1).astype(o_ref.dtype)


def _attn_call(qh, kv_sources, bound, tq):
    b, _, t, _ = qh.shape
    in_specs = [pl.BlockSpec(memory_space=pltpu.SMEM),
                pl.BlockSpec((1, ATTN_REP, tq, HEAD_DIM), lambda bi, g, i: (bi, g, i, 0))]
    args = [bound.reshape(1), qh]
    for kh, vh in kv_sources:
        tk = kh.shape[2]
        in_specs += [pl.BlockSpec((1, 1, tk, HEAD_DIM), lambda bi, g, i: (bi, g, 0, 0)),
                     pl.BlockSpec((1, 1, tk, LANES), lambda bi, g, i: (bi, g, 0, 0))]
        args += [kh, vh]

    def call(bounded):
        return pl.pallas_call(
            functools.partial(_attn_kernel, bounded=bounded),
            grid=(b, KV_HEADS, t // tq),
            in_specs=in_specs,
            out_specs=pl.BlockSpec((1, tq, ATTN_REP * HEAD_DIM), lambda bi, g, i: (bi, i, g)),
            out_shape=jax.ShapeDtypeStruct((b, t, ATTN_WIDTH), BF16),
            compiler_params=_cparams("parallel", "parallel", "parallel"),
            name="attn_bounded" if bounded else "attn_rowmax",
        )(*args)

    return lax.cond(bound <= ATTN_BOUND_LIMIT, lambda: call(True), lambda: call(False))


def _conv3(u, prev_row, next_row, w):
    tm, s = u.shape[0], 8
    rid = lax.broadcasted_iota(I32, (s, u.shape[1]), 0)
    up = pltpu.roll(u, 1, 0)
    up = jnp.concatenate([jnp.where(rid == 0, prev_row, up[:s]), up[s:]], axis=0)
    dn = pltpu.roll(u, tm - 1, 0)
    dn = jnp.concatenate([dn[:tm - s], jnp.where(rid == s - 1, next_row, dn[tm - s:])], axis=0)
    return up * w[0:1, :] + u * w[1:2, :] + dn * w[2:3, :]


def _conv_kernel(x_ref, xp_ref, xn_ref, c_ref, cp_ref, cn_ref, s_ref, sp_ref, sn_ref, b_ref,
                 w_ref, bias_ref, wsc_ref, xo_ref, yo_ref):
    i, n = pl.program_id(1), pl.num_programs(1)
    keep_p = jnp.where(i > 0, 1.0, 0.0)
    keep_n = jnp.where(i < n - 1, 1.0, 0.0)
    last = BF16_ROWS - 1
    f = lambda r: r.astype(F32)
    u = f(x_ref[0])
    y = _conv3(u, f(xp_ref[0, last:last + 1, :]) * keep_p, f(xn_ref[0, 0:1, :]) * keep_n, w_ref[...]) + bias_ref[...]
    xo_ref[0] = _silu(y).astype(xo_ref.dtype)
    cs = f(c_ref[0]) * f(s_ref[0])
    cs_p = f(cp_ref[0, last:last + 1, :]) * f(sp_ref[0, last:last + 1, :]) * keep_p
    cs_n = f(cn_ref[0, 0:1, :]) * f(sn_ref[0, 0:1, :]) * keep_n
    yo_ref[0] = (f(b_ref[0]) * _conv3(cs, cs_p, cs_n, wsc_ref[...])).astype(yo_ref.dtype)


def _conv_call(main, conv_w, conv_b, sc_w, tm):
    b, t, _ = main.shape
    hb = tm // BF16_ROWS
    nh = t // BF16_ROWS

    def centre(width, col):
        return pl.BlockSpec((1, tm, width), lambda bi, i: (bi, i, col // width))

    def prev(width, col):
        return pl.BlockSpec((1, BF16_ROWS, width), lambda bi, i: (bi, jnp.maximum(i * hb - 1, 0), col // width))

    def nxt(width, col):
        return pl.BlockSpec((1, BF16_ROWS, width), lambda bi, i: (bi, jnp.minimum((i + 1) * hb, nh - 1), col // width))

    const = lambda shape: pl.BlockSpec(shape, lambda bi, i: (0, 0))
    return pl.pallas_call(
        _conv_kernel,
        grid=(b, t // tm),
        in_specs=[centre(SSD_CONV_DIM, COL_XBC), prev(SSD_CONV_DIM, COL_XBC), nxt(SSD_CONV_DIM, COL_XBC),
                  centre(SC_WIDTH, COL_SCC), prev(SC_WIDTH, COL_SCC), nxt(SC_WIDTH, COL_SCC),
                  centre(SC_WIDTH, COL_SCX), prev(SC_WIDTH, COL_SCX), nxt(SC_WIDTH, COL_SCX),
                  centre(SC_WIDTH, COL_SCB),
                  const((3, SSD_CONV_DIM)), const((1, SSD_CONV_DIM)), const((3, SC_WIDTH))],
        out_specs=[pl.BlockSpec((1, tm, SSD_CONV_DIM), lambda bi, i: (bi, i, 0)),
                   pl.BlockSpec((1, tm, SC_WIDTH), lambda bi, i: (bi, i, 0))],
        out_shape=[jax.ShapeDtypeStruct((b, t, SSD_CONV_DIM), BF16),
                   jax.ShapeDtypeStruct((b, t, SC_WIDTH), BF16)],
        compiler_params=_cparams("parallel", "parallel"),
        name="conv",
    )(main, main, main, main, main, main, main, main, main, main, conv_w, conv_b, sc_w)


def _ssd_kernel(xa_ref, dt_ref, dtT_ref, z_ref, s0_ref, brow_ref, arow_ref, bcol_ref, acol_ref, dexp_ref, nw_ref,
                y_ref, so_ref, yf_ref, st_ref):
    L = SSD_CHUNK
    t = xa_ref.shape[1]
    nc = t // L
    ri = lax.broadcasted_iota(I32, (L, L), 0)
    ci = lax.broadcasted_iota(I32, (L, L), 1)
    lane512 = lax.broadcasted_iota(I32, (L, SSD_INNER), 1)
    row512 = lax.broadcasted_iota(I32, (L, SSD_INNER), 0)
    blockmask = jnp.where((row512 >> 6) == (lane512 >> 8), 1.0, 0.0)
    lane128 = lax.broadcasted_iota(I32, (L, LANES), 1)
    a_row = -jnp.exp(arow_ref[...])
    a_col = -jnp.exp(acol_ref[...])

    st_ref[...] = s0_ref[0]

    def chunk(c, d):
        r0 = pl.multiple_of(c * L, L)
        causal = (ri >= ci) if d == 0 else (ri <= ci)
        tri = jnp.where(causal, 1.0, 0.0).astype(BF16)
        triT = jnp.where((ri <= ci) if d == 0 else (ri >= ci), 1.0, 0.0).astype(BF16)
        expand = jnp.where(row512 == (lane512 >> 6) + d * SSD_HEADS, 1.0, 0.0).astype(BF16)
        xa = xa_ref[0, pl.ds(r0, L), :]
        x = xa[:, :SSD_INNER]
        b_pair = xa[:, SSD_INNER:SSD_INNER + SSD_GN]
        c_pair = xa[:, SSD_INNER + SSD_GN:]
        dt_all = _softplus(dt_ref[0, pl.ds(r0, L), :] + brow_ref[...])
        a1, a2, a3 = _split3(dt_all * a_row)
        acs = _dot(tri, a1) + _dot(tri, a2) + _dot(tri, a3)
        dtT = _softplus(dtT_ref[0, :, pl.ds(r0, L)] + bcol_ref[...])
        acsT = _dot_exact_lhs(dtT * a_col, triT)
        edge = acs[L - 1:L, :] if d == 0 else acs[0:1, :]
        w_all = jnp.exp(edge - acs) * dt_all
        e_exp = jnp.exp(_dot_exact_lhs(acs, expand))
        w_exp = _dot(w_all.astype(BF16), expand)
        state = st_ref[d]
        y_off = _dot(c_pair, state.astype(BF16)) * e_exp
        gmats = [lax.dot_general(jnp.where((lane128 >> 6) == g, c_pair, jnp.zeros_like(c_pair)), b_pair, _NT,
                                 preferred_element_type=F32) for g in range(SSD_GROUPS)]
        pairs = []
        for pr in range(SSD_HEADS // 2):
            x_pair = x[:, pr * LANES:(pr + 1) * LANES]
            acc = jnp.zeros((L, LANES), F32)
            for hh in range(2):
                h = 2 * pr + hh
                gmat = gmats[h // (SSD_HEADS // SSD_GROUPS)]
                k = d * SSD_HEADS + h
                seg = acs[:, k:k + 1] - acsT[k:k + 1, :]
                m = jnp.where(causal, jnp.exp(jnp.where(causal, seg, 0.0)), 0.0) * gmat * dtT[k:k + 1, :]
                xh = jnp.where((lane128 >> 6) == hh, x_pair, jnp.zeros_like(x_pair))
                acc = acc + _dot(m.astype(BF16), xh)
            pairs.append(acc)
        y = y_off + jnp.concatenate(pairs, axis=1)
        e_edge = e_exp[L - 1:L, :] if d == 0 else e_exp[0:1, :]
        ds = lax.dot_general(b_pair, (x.astype(F32) * w_exp).astype(BF16), _TN, preferred_element_type=F32)
        st_ref[d] = (state * e_edge + ds) * blockmask
        return y, x, r0

    def fwd(c, carry):
        y, _, r0 = chunk(c, 0)
        yf_ref[pl.ds(r0, L), :] = y
        return carry

    unroll = 4 if nc % 4 == 0 else (2 if nc % 2 == 0 else 1)
    lax.fori_loop(0, nc, fwd, 0, unroll=unroll)

    def bwd(i, carry):
        c = nc - 1 - i
        yb, x, r0 = chunk(c, 1)
        y = yf_ref[pl.ds(r0, L), :] + yb + dexp_ref[...] * x.astype(F32)
        gated = y * _silu(z_ref[0, pl.ds(r0, L), :].astype(F32))
        ms = jnp.mean(gated * gated, axis=-1, keepdims=True)
        y_ref[0, pl.ds(r0, L), :] = (gated * lax.rsqrt(ms + EPS) * nw_ref[...]).astype(y_ref.dtype)
        return carry

    lax.fori_loop(0, nc, bwd, 0, unroll=unroll)
    so_ref[0] = st_ref[...]


def _ssd_call(xa, dt, dtT, main, s0, brow, arow, bcol, acol, dexp, nw):
    b, t, _ = xa.shape
    c2 = lambda shape: pl.BlockSpec(shape, lambda bi: (0, 0))
    return pl.pallas_call(
        _ssd_kernel,
        grid=(b,),
        in_specs=[pl.BlockSpec((1, t, SSD_CONV_DIM), lambda bi: (bi, 0, 0)),
                  pl.BlockSpec((1, t, LANES), lambda bi: (bi, 0, 0)),
                  pl.BlockSpec((1, 2 * SSD_HEADS, t), lambda bi: (bi, 0, 0)),
                  pl.BlockSpec((1, t, SSD_INNER), lambda bi: (bi, 0, COL_Z // SSD_INNER)),
                  pl.BlockSpec((1, 2, SSD_GN, SSD_INNER), lambda bi: (bi, 0, 0, 0)),
                  c2((1, LANES)), c2((1, LANES)), c2((2 * SSD_HEADS, LANES)), c2((2 * SSD_HEADS, LANES)),
                  c2((1, SSD_INNER)), c2((1, SSD_INNER))],
        out_specs=[pl.BlockSpec((1, t, SSD_INNER), lambda bi: (bi, 0, 0)),
                   pl.BlockSpec((1, 2, SSD_GN, SSD_INNER), lambda bi: (bi, 0, 0, 0))],
        out_shape=[jax.ShapeDtypeStruct((b, t, SSD_INNER), BF16),
                   jax.ShapeDtypeStruct((b, 2, SSD_GN, SSD_INNER), F32)],
        scratch_shapes=[pltpu.VMEM((t, SSD_INNER), F32), pltpu.VMEM((2, SSD_GN, SSD_INNER), F32)],
        compiler_params=_cparams("parallel"),
        name="ssd",
    )(xa, dt, dtT, main, s0, brow, arow, bcol, acol, dexp, nw)


def _merge_kernel(ya_ref, ys_ref, yc_ref, g_ref, x_ref, m2_ref, m3_ref, m4_ref, wa_ref, ws_ref, wc_ref, wo_ref,
                  lg_ref, lb_ref, wr_ref, x1_ref, h2_ref, aff_ref, *, alpha):
    d = x_ref.shape[2]
    gate = lambda j: jax.nn.sigmoid(g_ref[0, :, j * d:(j + 1) * d].astype(F32))
    m = (gate(0) * _dot(ya_ref[0], wa_ref[...]) + gate(1) * _dot(ys_ref[0], ws_ref[...])
         + gate(2) * _dot(yc_ref[0], wc_ref[...]))
    out = _dot(m.astype(BF16), wo_ref[...])
    x1 = _layer_norm(alpha * x_ref[0] + m2_ref[0] * out) * lg_ref[...] + lb_ref[...]
    x1_ref[0] = x1
    h2 = _layer_norm(x1) * (1.0 + m4_ref[0]) + m3_ref[0]
    h2_ref[0] = h2.astype(h2_ref.dtype)
    logits = _dot_hp_nt(wr_ref[...], h2)
    e = jnp.exp(logits - jnp.max(logits, axis=0, keepdims=True))
    aff_ref[0] = e / jnp.sum(e, axis=0, keepdims=True)


def _merge_call(ya, ys, yc, main, x, m2, m3, m4, wa, ws, wc, wo, lg, lb, wrT, alpha, tm):
    b, t, d = x.shape
    tok = lambda w: pl.BlockSpec((1, tm, w), lambda bi, i: (bi, i, 0))
    per_b = pl.BlockSpec((1, 1, d), lambda bi, i: (bi, 0, 0))
    const = lambda shape: pl.BlockSpec(shape, lambda bi, i: (0, 0))
    return pl.pallas_call(
        functools.partial(_merge_kernel, alpha=alpha),
        grid=(b, t // tm),
        in_specs=[tok(ATTN_WIDTH), tok(SSD_INNER), tok(SC_WIDTH), tok(N_BRANCH * d), tok(d),
                  per_b, per_b, per_b,
                  const((ATTN_WIDTH, d)), const((SSD_INNER, d)), const((SC_WIDTH, d)), const((d, d)),
                  const((1, d)), const((1, d)), const((N_EXPERTS, d))],
        out_specs=[tok(d), tok(d), pl.BlockSpec((1, N_EXPERTS, tm), lambda bi, i: (bi, 0, i))],
        out_shape=[jax.ShapeDtypeStruct((b, t, d), F32), jax.ShapeDtypeStruct((b, t, d), BF16),
                   jax.ShapeDtypeStruct((b, N_EXPERTS, t), F32)],
        compiler_params=_cparams("parallel", "parallel"),
        name="merge",
    )(ya, ys, yc, main, x, m2, m3, m4, wa, ws, wc, wo, lg, lb, wrT)


def _route_kernel(aff_ref, pos_ref, w_ref, offs_ref, *, cap):
    aff = aff_ref[0]
    ne, t = aff.shape
    capf = float(cap)

    def step(i, lo):
        cand = lo | jnp.left_shift(jnp.int32(1), 30 - i)
        cnt = jnp.sum(jnp.where(aff >= pltpu.bitcast(cand, F32), 1.0, 0.0), axis=1, keepdims=True)
        return jnp.where(cnt >= capf, cand, lo)

    thr = pltpu.bitcast(lax.fori_loop(0, 31, step, jnp.zeros((ne, 1), I32)), F32)
    need = capf - jnp.sum(jnp.where(aff > thr, 1.0, 0.0), axis=1, keepdims=True)
    ri = lax.broadcasted_iota(I32, (LANES, LANES), 0)
    ci = lax.broadcasted_iota(I32, (LANES, LANES), 1)
    upper = jnp.where(ri < ci, 1.0, 0.0).astype(BF16)
    lane = lax.broadcasted_iota(I32, (ne, LANES), 1)
    carry_gt = jnp.zeros((ne, 1), F32)
    carry_eq = jnp.zeros((ne, 1), F32)
    offs = jnp.zeros((ne, LANES), F32)
    nblk = t // LANES
    for j in range(nblk):
        sl = slice(j * LANES, (j + 1) * LANES)
        aj = aff[:, sl]
        gj = jnp.where(aj > thr, 1.0, 0.0)
        ej = jnp.where(aj == thr, 1.0, 0.0)
        pre = _dot(jnp.concatenate([gj, ej], axis=0).astype(BF16), upper)
        pre_gt = pre[:ne] + carry_gt
        pre_eq = pre[ne:] + carry_eq
        sel = gj + ej * jnp.where(pre_eq < need, 1.0, 0.0)
        slot = pre_gt + jnp.minimum(pre_eq, need)
        pos_ref[0, :, sl] = jnp.where(sel > 0.5, slot, -1.0)
        w_ref[0, :, sl] = jnp.where(sel > 0.5, aj, 0.0)
        offs = jnp.where(lane == j, carry_gt + jnp.minimum(carry_eq, need), offs)
        carry_gt = carry_gt + jnp.sum(gj, axis=1, keepdims=True)
        carry_eq = carry_eq + jnp.sum(ej, axis=1, keepdims=True)
    offs = jnp.where(lane >= nblk, capf, offs)
    offs_ref[0] = offs.astype(I32)


def _route_call(affT, cap):
    b, ne, t = affT.shape
    spec = pl.BlockSpec((1, ne, t), lambda bi: (bi, 0, 0))
    return pl.pallas_call(
        functools.partial(_route_kernel, cap=cap),
        grid=(b,),
        in_specs=[spec],
        out_specs=[spec, spec, pl.BlockSpec((1, ne, LANES), lambda bi: (bi, 0, 0))],
        out_shape=[jax.ShapeDtypeStruct((b, ne, t), F32), jax.ShapeDtypeStruct((b, ne, t), F32),
                   jax.ShapeDtypeStruct((b, ne, LANES), I32)],
        compiler_params=_cparams("parallel"),
        name="route",
    )(affT)


EXPERT_GROUP = 4


def _slot_windows(offs_ref, bi, i, ne, per, cap, win):
    out = []
    for e in range(ne):
        lo = offs_ref[(bi * ne + e) * LANES + i * per]
        hi = offs_ref[(bi * ne + e) * LANES + (i + 1) * per]
        out.append((lo, hi, lo - lax.rem(lo, BF16_ROWS)))
    return out


def _window_start(wb, p, cap, win):
    return pl.multiple_of(jnp.minimum(wb + p * win, cap - win), BF16_ROWS)


def _gather_kernel(offs_ref, h_ref, pos_ref, *rest, win):
    xs_ref = rest[-1]
    bi, i = pl.program_id(0), pl.program_id(1)
    tm = h_ref.shape[1]
    ne, cap = xs_ref.shape[1], xs_ref.shape[2]
    windows = _slot_windows(offs_ref, bi, i, ne, tm // LANES, cap, win)
    h = h_ref[0]
    pos = pos_ref[0]
    row = lax.broadcasted_iota(I32, (win, tm), 0)

    @pl.when(i == 0)
    def _():
        xs_ref[...] = jnp.zeros_like(xs_ref)

    def onehot(e, start, first=None):
        slot = row + start
        if first is not None:
            slot = jnp.where(slot >= first, slot, -2)
        return jnp.where(pos[e:e + 1, :] == slot.astype(F32), 1.0, 0.0).astype(BF16)

    def add_window(e, start, rows):
        xs_ref[0, e, pl.ds(start, win), :] += rows.astype(xs_ref.dtype)

    for g0 in range(0, ne, EXPERT_GROUP):
        experts = range(g0, min(g0 + EXPERT_GROUP, ne))
        starts = [_window_start(windows[e][2], 0, cap, win) for e in experts]
        y = _dot(jnp.concatenate([onehot(e, s) for e, s in zip(experts, starts)], axis=0), h)
        for k, (e, s) in enumerate(zip(experts, starts)):
            add_window(e, s, y[k * win:(k + 1) * win])
    for p in range(1, pl.cdiv(cap, win)):
        for e in range(ne):
            lo, hi, wb = windows[e]

            @pl.when(hi > wb + p * win)
            def _():
                s = _window_start(wb, p, cap, win)
                add_window(e, s, _dot(onehot(e, s, first=wb + p * win), h))


def _gather_call(offs, h2, pos, cap, tm, slots_total, slot_base, buf=None):
    b, t, d = h2.shape
    ne = pos.shape[1]
    mean = tm * cap // t
    win = min(LANES, cap, pl.cdiv(mean + mean // 4 + BF16_ROWS, BF16_ROWS) * BF16_ROWS)
    in_specs = [pl.BlockSpec((1, tm, d), lambda bi, i, offs: (bi, i, 0)),
                pl.BlockSpec((1, ne, tm), lambda bi, i, offs: (bi, 0, i))]
    args = [offs, h2, pos]
    if buf is not None:
        in_specs.append(pl.BlockSpec(memory_space=pl.ANY))
        args.append(buf)
    return pl.pallas_call(
        functools.partial(_gather_kernel, win=win),
        grid_spec=pltpu.PrefetchScalarGridSpec(
            num_scalar_prefetch=1,
            grid=(b, t // tm),
            in_specs=in_specs,
            out_specs=pl.BlockSpec((1, ne, cap, d), lambda bi, i, offs: (bi, 0, slot_base // cap, 0))),
        out_shape=jax.ShapeDtypeStruct((b, ne, slots_total, d), BF16),
        input_output_aliases={} if buf is None else {3: 0},
        compiler_params=_cparams("parallel", "arbitrary"),
        name="gather",
    )(*args)


def _ffn_kernel(xs_ref, wg_ref, wu_ref, wd_ref, o_ref, acc_ref, *, group):
    f, nf = pl.program_id(1), pl.num_programs(1)
    b, _, slots, d = xs_ref.shape
    kc = 256

    def dot_w(a, w_ref):
        k = w_ref.shape[2]
        return sum(_dot(a[:, k0:k0 + kc], w_ref[0, 0, k0:k0 + kc, :].astype(BF16)) for k0 in range(0, k, kc))

    @pl.when(f == 0)
    def _():
        acc_ref[...] = jnp.zeros_like(acc_ref)

    for b0 in range(0, b, group):
        x = xs_ref[b0:b0 + group, 0].reshape(group * slots, d)
        hid = (_silu(dot_w(x, wg_ref)) * dot_w(x, wu_ref)).astype(BF16)
        acc_ref[b0:b0 + group] += dot_w(hid, wd_ref).reshape(group, slots, d)

    @pl.when(f == nf - 1)
    def _():
        o_ref[:, 0] = acc_ref[...].astype(o_ref.dtype)


def _ffn_call(xs, wg, wu, wd, layer, tf):
    b, ne, slots, d = xs.shape
    ff = wg.shape[3]
    tok = pl.BlockSpec((b, 1, slots, d), lambda e, f: (0, e, 0, 0))
    return pl.pallas_call(
        functools.partial(_ffn_kernel, group=1),
        grid=(ne, ff // tf),
        in_specs=[tok,
                  pl.BlockSpec((1, 1, d, tf), lambda e, f: (layer, e, 0, f)),
                  pl.BlockSpec((1, 1, d, tf), lambda e, f: (layer, e, 0, f)),
                  pl.BlockSpec((1, 1, tf, d), lambda e, f: (layer, e, f, 0))],
        out_specs=tok,
        out_shape=jax.ShapeDtypeStruct(xs.shape, BF16),
        scratch_shapes=[pltpu.VMEM((b, slots, d), F32)],
        compiler_params=_cparams("parallel", "arbitrary"),
        name="ffn",
    )(xs, wg, wu, wd)


def _combine_kernel(offs_ref, o_ref, pos_ref, w_ref, x_ref, m5_ref, lg_ref, lb_ref, y_ref, acc_ref, *, alpha, win):
    bi, i = pl.program_id(0), pl.program_id(1)
    tm = x_ref.shape[1]
    ne, cap = o_ref.shape[1], o_ref.shape[2]
    windows = _slot_windows(offs_ref, bi, i, ne, tm // LANES, cap, win)
    pos = pos_ref[0]
    w = w_ref[0]
    col = lax.broadcasted_iota(I32, (tm, win), 1)

    def onehot(e, start, first=None):
        slot = col + start
        if first is not None:
            slot = jnp.where(slot >= first, slot, -2)
        return jnp.where(pos[:, e:e + 1] == slot.astype(F32), w[:, e:e + 1], 0.0).astype(BF16)

    y = jnp.zeros((tm, o_ref.shape[3]), F32)
    for g0 in range(0, ne, EXPERT_GROUP):
        experts = range(g0, min(g0 + EXPERT_GROUP, ne))
        starts = [_window_start(windows[e][2], 0, cap, win) for e in experts]
        lhs = jnp.concatenate([onehot(e, s) for e, s in zip(experts, starts)], axis=1)
        rhs = jnp.concatenate([o_ref[0, e, pl.ds(s, win), :] for e, s in zip(experts, starts)], axis=0)
        y = y + _dot(lhs, rhs)
    acc_ref[...] = y
    for p in range(1, cap // win):
        for e in range(ne):
            lo, hi, wb = windows[e]

            @pl.when(hi > wb + p * win)
            def _():
                s = _window_start(wb, p, cap, win)
                acc_ref[...] += _dot(onehot(e, s, first=wb + p * win), o_ref[0, e, pl.ds(s, win), :])
    y_ref[0] = _layer_norm(alpha * x_ref[0] + m5_ref[0] * acc_ref[...]) * lg_ref[...] + lb_ref[...]


def _combine_call(offs, out, posT, wT, x1, m5, lg, lb, alpha, tm, cap, slot_base):
    b, t, d = x1.shape
    ne = out.shape[1]
    return pl.pallas_call(
        functools.partial(_combine_kernel, alpha=alpha, win=min(LANES, cap)),
        grid_spec=pltpu.PrefetchScalarGridSpec(
            num_scalar_prefetch=1,
            grid=(b, t // tm),
            in_specs=[pl.BlockSpec((1, ne, cap, d), lambda bi, i, offs: (bi, 0, slot_base // cap, 0)),
                      pl.BlockSpec((1, tm, ne), lambda bi, i, offs: (bi, i, 0)),
                      pl.BlockSpec((1, tm, ne), lambda bi, i, offs: (bi, i, 0)),
                      pl.BlockSpec((1, tm, d), lambda bi, i, offs: (bi, i, 0)),
                      pl.BlockSpec((1, 1, d), lambda bi, i, offs: (bi, 0, 0)),
                      pl.BlockSpec((1, d), lambda bi, i, offs: (0, 0)),
                      pl.BlockSpec((1, d), lambda bi, i, offs: (0, 0))],
            out_specs=pl.BlockSpec((1, tm, d), lambda bi, i, offs: (bi, i, 0)),
            scratch_shapes=[pltpu.VMEM((tm, d), F32)]),
        out_shape=jax.ShapeDtypeStruct((b, t, d), F32),
        compiler_params=_cparams("parallel", "arbitrary"),
        name="combine",
    )(offs, out, posT, wT, x1, m5, lg, lb)


def _rope_tables(n):
    rows = n // GRID_W
    row = jnp.repeat(jnp.arange(rows), GRID_W).astype(F32)
    col = jnp.tile(jnp.arange(GRID_W), rows).astype(F32)
    inv = ROPE_THETA ** (-jnp.arange(0, AXIS_ROT, 2, dtype=F32) / AXIS_ROT)
    ang = jnp.stack([row[:, None] * inv, col[:, None] * inv], axis=1)
    cos = jnp.repeat(jnp.cos(ang)[:, :, None, :], 2, axis=2).reshape(n, HEAD_DIM)
    sin = jnp.stack([-jnp.sin(ang), jnp.sin(ang)], axis=2).reshape(n, HEAD_DIM)
    return jnp.tile(cos, (1, LANES // HEAD_DIM)), jnp.tile(sin, (1, LANES // HEAD_DIM))


def _tile(t, pref):
    return pref if t % pref == 0 else t


def _moe(streams, wg, wu, wd, layer, lg, lb, alpha):
    caps = [CAPACITY_FACTOR * h2.shape[1] // N_EXPERTS for _, h2, _, _ in streams]
    bases = [sum(caps[:k]) for k in range(len(caps))]
    assert all(base % cap == 0 for base, cap in zip(bases, caps)) and sum(caps) % BF16_ROWS == 0
    xs, routed = None, []
    for (_, h2, affT, _), cap, base in zip(streams, caps, bases):
        t = h2.shape[1]
        pos, wsel, offs = _route_call(affT, cap)
        offs = offs.reshape(-1)
        xs = _gather_call(offs, h2, pos, cap, tm=_tile(t, 512), slots_total=sum(caps), slot_base=base, buf=xs)
        routed.append((offs, jnp.swapaxes(pos, 1, 2), jnp.swapaxes(wsel, 1, 2)))
    out = _ffn_call(xs, wg, wu, wd, layer, tf=_tile(wg.shape[3], 512))
    return [_combine_call(offs, out, posT, wT, x1, m5, lg, lb, alpha, tm=_tile(x1.shape[1], 512),
                          cap=cap, slot_base=base)
            for (x1, _, _, m5), (offs, posT, wT), cap, base in zip(streams, routed, caps, bases)]


def kernel(x, c, ctx, c_ctx, w_mod, b_mod, w_in, q_norm, k_norm, ssd_conv_w, ssd_conv_b, ssd_a_log, ssd_dt_bias,
           ssd_d, ssd_norm, sc_conv_w, w_br_attn, w_br_ssd, w_br_conv, w_o, ln1_g, ln1_b, w_router, w_exp_gate,
           w_exp_up, w_exp_down, ln2_g, ln2_b):
    b, n, d = x.shape
    nctx = ctx.shape[1]
    depth = w_mod.shape[0]
    alpha = (2 * depth) ** 0.25

    rows = -(-(b + 1) // 8) * 8
    cvec = jnp.concatenate([c, c_ctx[None], jnp.zeros((rows - b - 1, d), F32)], axis=0)
    mods = _mod_call(cvec, w_mod, b_mod)

    cos_l, sin_l = _rope_tables(n)
    cos_c, sin_c = jnp.ones((nctx, LANES), F32), jnp.zeros((nctx, LANES), F32)
    hid = jnp.arange(ATTN_WIDTH) // HEAD_DIM
    bd = (hid[:, None] == hid[None, :]).astype(BF16)
    dexp_all = jnp.repeat(ssd_d, SSD_HEAD_DIM, axis=1)
    pad16 = lambda v: jnp.pad(v.reshape(1, 2 * SSD_HEADS), ((0, 0), (0, LANES - 2 * SSD_HEADS)))
    col16 = lambda v: jnp.broadcast_to(v.reshape(2 * SSD_HEADS, 1), (2 * SSD_HEADS, LANES))

    w_main, w_dt = _wprep_call(w_in)
    x_ctx = ctx
    for i in range(depth):
        need_ctx = i < depth - 1
        gq = jnp.tile(q_norm[i], ATTN_HEADS)[None]
        gk = jnp.tile(k_norm[i], KV_HEADS)[None]
        brow, arow = pad16(ssd_dt_bias[i]), pad16(ssd_a_log[i])
        bcol, acol = col16(ssd_dt_bias[i]), col16(ssd_a_log[i])
        dexp, nw = dexp_all[i][None], ssd_norm[i][None]
        wa, ws, wc, wo = (w.astype(BF16) for w in (w_br_attn[i], w_br_ssd[i], w_br_conv[i], w_o[i]))
        wrT = w_router[i].T
        lg1, lb1, lg2, lb2 = ln1_g[i][None], ln1_b[i][None], ln2_g[i][None], ln2_b[i][None]
        mod_l = [m[:, None, :] for m in jnp.split(mods[i, :b], N_MOD, axis=-1)]
        mod_c = [jnp.broadcast_to(m[None, None, :], (b, 1, d)) for m in jnp.split(mods[i, b], N_MOD, axis=-1)]

        def mixer(xin, mod, cos, sin, rope, s0):
            t = xin.shape[1]
            main, dt, dtT = _inproj_call(xin, mod[0], mod[1], w_main, i, w_dt, tm=_tile(t, 512))
            qh, kh, vh = _qkprep_call(main, cos, sin, gq, gk, bd, tm=_tile(t, 512), rope=rope)
            xa, yc = _conv_call(main, ssd_conv_w[i], ssd_conv_b[i][None], sc_conv_w[i], tm=_tile(t, 512))
            ys, s_out = _ssd_call(xa, dt, dtT, main, s0, brow, arow, bcol, acol, dexp, nw)
            return main, qh, kh, vh, yc, ys, s_out

        def merge(xin, mod, main, ya, ys, yc):
            return _merge_call(ya, ys, yc, main, xin, mod[2], mod[3], mod[4], wa, ws, wc, wo, lg1, lb1, wrT,
                               alpha, tm=_tile(xin.shape[1], 512))

        zero_state = jnp.zeros((b, 2, SSD_GN, SSD_INNER), F32)
        main_c, qh_c, kh_c, vh_c, yc_c, ys_c, s_ctx = mixer(x_ctx, mod_c, cos_c, sin_c, False, zero_state)
        main_l, qh_l, kh_l, vh_l, yc_l, ys_l, _ = mixer(x, mod_l, cos_l, sin_l, True, s_ctx)
        bound = math.sqrt(HEAD_DIM) * jnp.max(jnp.abs(q_norm[i])) * jnp.max(jnp.abs(k_norm[i]))
        ya_l = _attn_call(qh_l, [(kh_c, vh_c), (kh_l, vh_l)], bound, tq=_tile(n, 256))
        streams = [(*merge(x, mod_l, main_l, ya_l, ys_l, yc_l), mod_l[5])]
        if need_ctx:
            ya_c = _attn_call(qh_c, [(kh_c, vh_c)], bound, tq=_tile(nctx, 256))
            streams.append((*merge(x_ctx, mod_c, main_c, ya_c, ys_c, yc_c), mod_c[5]))
        new = _moe(streams, w_exp_gate, w_exp_up, w_exp_down, i, lg2, lb2, alpha)
        x = new[0]
        if need_ctx:
            x_ctx = new[1]
    return x
```

```python
import functools
import math

import jax
import jax.numpy as jnp
from jax import lax
from jax.experimental import pallas as pl
from jax.experimental.pallas import tpu as pltpu

F32, BF16, I32 = jnp.float32, jnp.bfloat16, jnp.int32

HEAD_DIM = 64
ATTN_HEADS = 8
KV_HEADS = 2
ATTN_REP = ATTN_HEADS // KV_HEADS
ATTN_WIDTH = ATTN_HEADS * HEAD_DIM
KV_WIDTH = KV_HEADS * HEAD_DIM
AXIS_ROT = HEAD_DIM // 2
ROPE_THETA = 10000.0
GRID_W = 64
SSD_HEADS = 8
SSD_HEAD_DIM = 64
SSD_INNER = SSD_HEADS * SSD_HEAD_DIM
SSD_STATE = 64
SSD_GROUPS = 2
SSD_GN = SSD_GROUPS * SSD_STATE
SSD_CONV_DIM = SSD_INNER + 2 * SSD_GN
SSD_CHUNK = 128
SC_WIDTH = 512
N_BRANCH = 3
N_EXPERTS = 16
CAPACITY_FACTOR = 2
N_MOD = 6
EPS = 1e-6

LANES = 128
BF16_ROWS = 16
VMEM_LIMIT_BYTES = 56 * 1024 * 1024

COL_GATE, COL_XBC, COL_KV, COL_Q, COL_Z, COL_SCB, COL_SCC, COL_SCX = 0, 3072, 3840, 4096, 4608, 5120, 5632, 6144
MAIN_COLS = 6656

_NT = (((1,), (1,)), ((), ()))
_TN = (((0,), (0,)), ((), ()))


def _cparams(*sem):
    return pltpu.CompilerParams(dimension_semantics=sem, vmem_limit_bytes=VMEM_LIMIT_BYTES)


def _dot(a, b):
    return jnp.dot(a, b, preferred_element_type=F32)


def _split2(a):
    hi = a.astype(BF16)
    lo = (a - hi.astype(F32)).astype(BF16)
    return hi, lo


def _split3(a):
    p1 = a.astype(BF16)
    r1 = a - p1.astype(F32)
    p2 = r1.astype(BF16)
    p3 = (r1 - p2.astype(F32)).astype(BF16)
    return p1, p2, p3


def _dot_hp(a, w):
    ah, al = _split2(a)
    wh, wl = _split2(w)
    return _dot(ah, wh) + _dot(al, wh) + _dot(ah, wl)


def _dot_hp_nt(a, w):
    ah, al = _split2(a)
    wh, wl = _split2(w)
    d = lambda x, y: lax.dot_general(x, y, _NT, preferred_element_type=F32)
    return d(ah, wh) + d(al, wh) + d(ah, wl)


def _dot_exact_lhs(a, m):
    p1, p2, p3 = _split3(a)
    return _dot(p1, m) + _dot(p2, m) + _dot(p3, m)


def _layer_norm(x):
    mu = jnp.mean(x, axis=-1, keepdims=True)
    xc = x - mu
    var = jnp.mean(xc * xc, axis=-1, keepdims=True)
    return xc * lax.rsqrt(var + EPS)


def _softplus(x):
    return jnp.maximum(x, 0.0) + jnp.log(1.0 + jnp.exp(-jnp.abs(x)))


def _silu(x):
    return x * jax.nn.sigmoid(x)


def _mod_kernel(c_ref, w_ref, b_ref, o_ref):
    o_ref[0] = _dot_hp(_silu(c_ref[...]), w_ref[0]) + b_ref[0]


def _mod_call(cvec, w_mod, b_mod):
    depth, d, n = w_mod.shape
    rows = cvec.shape[0]
    tn = 1536
    return pl.pallas_call(
        _mod_kernel,
        grid=(depth, n // tn),
        in_specs=[pl.BlockSpec((rows, d), lambda l, j: (0, 0)),
                  pl.BlockSpec((1, d, tn), lambda l, j: (l, 0, j)),
                  pl.BlockSpec((1, 1, tn), lambda l, j: (l, 0, j))],
        out_specs=pl.BlockSpec((1, rows, tn), lambda l, j: (l, 0, j)),
        out_shape=jax.ShapeDtypeStruct((depth, rows, n), F32),
        compiler_params=_cparams("parallel", "parallel"),
        name="mod",
    )(cvec, w_mod, b_mod.reshape(depth, 1, n))


_REF_COLS = dict(k=(0, 128), v=(128, 256), xbc=(256, 1024), dt=(1024, 1040), q=(1040, 1552), z=(1552, 2064),
                 scb=(2064, 2576), scc=(2576, 3088), scx=(3088, 3600), gate=(3600, 6672))
_MAIN_ORDER = ("gate", "xbc", "k", "v", "q", "z", "scb", "scc", "scx")


def _inproj_kernel(x_ref, sh_ref, sc_ref, w_ref, wdt_ref, o_ref, dt_ref, dtT_ref):
    h = _layer_norm(x_ref[0]) * (1.0 + sc_ref[0]) + sh_ref[0]
    hb = h.astype(BF16)
    step, c0 = 512, 0
    for name in _MAIN_ORDER:
        lo, hi = _REF_COLS[name]
        for r0 in range(lo, hi, step):
            r1 = min(r0 + step, hi)
            y = lax.dot_general(hb, w_ref[0, r0:r1, :], _NT, preferred_element_type=F32)
            o_ref[0, :, c0:c0 + r1 - r0] = y.astype(o_ref.dtype)
            c0 += r1 - r0
    dt = _dot_hp_nt(h, wdt_ref[0])
    dt_ref[0] = dt
    dtT_ref[0] = dt.T[:2 * SSD_HEADS, :]


def _inproj_call(x, shift, scale, w_t, layer, w_dt_t, tm):
    b, t, d = x.shape
    return pl.pallas_call(
        _inproj_kernel,
        grid=(b, t // tm),
        in_specs=[pl.BlockSpec((1, tm, d), lambda bi, i: (bi, i, 0)),
                  pl.BlockSpec((1, 1, d), lambda bi, i: (bi, 0, 0)),
                  pl.BlockSpec((1, 1, d), lambda bi, i: (bi, 0, 0)),
                  pl.BlockSpec((1, w_t.shape[1], d), lambda bi, i: (layer, 0, 0)),
                  pl.BlockSpec((1, LANES, d), lambda bi, i: (layer, 0, 0))],
        out_specs=[pl.BlockSpec((1, tm, MAIN_COLS), lambda bi, i: (bi, i, 0)),
                   pl.BlockSpec((1, tm, LANES), lambda bi, i: (bi, i, 0)),
                   pl.BlockSpec((1, 2 * SSD_HEADS, tm), lambda bi, i: (bi, 0, i))],
        out_shape=[jax.ShapeDtypeStruct((b, t, MAIN_COLS), BF16),
                   jax.ShapeDtypeStruct((b, t, LANES), F32),
                   jax.ShapeDtypeStruct((b, 2 * SSD_HEADS, t), F32)],
        compiler_params=_cparams("parallel", "parallel"),
        name="inproj",
    )(x, shift, scale, w_t, w_dt_t)


def _head_rms(x, bd, g):
    hi, lo = _split2(x * x)
    ss = _dot(hi, bd) + _dot(lo, bd)
    return x * lax.rsqrt(ss * (1.0 / HEAD_DIM) + EPS) * g


def _rope(x, cos, sin_signed):
    half = AXIS_ROT // 2
    outs = []
    for j in range(x.shape[1] // LANES):
        xj = x[:, j * LANES:(j + 1) * LANES]
        lane = lax.broadcasted_iota(I32, xj.shape, 1)
        partner = jnp.where((lane & half) == 0, pltpu.roll(xj, LANES - half, 1), pltpu.roll(xj, half, 1))
        outs.append(xj * cos + partner * sin_signed)
    return outs[0] if len(outs) == 1 else jnp.concatenate(outs, axis=1)


def _qkprep_kernel(q_ref, kv_ref, cos_ref, sin_ref, gq_ref, gk_ref, bd_ref, qo_ref, ko_ref, vo_ref, *, rope):
    q = _head_rms(q_ref[0].astype(F32), bd_ref[...], gq_ref[...])
    kv = kv_ref[0]
    k = _head_rms(kv[:, :KV_WIDTH].astype(F32), bd_ref[:KV_WIDTH, :KV_WIDTH], gk_ref[...])
    if rope:
        q = _rope(q, cos_ref[...], sin_ref[...])
        k = _rope(k, cos_ref[...], sin_ref[...])
    q = q * (HEAD_DIM ** -0.5)
    for h in range(ATTN_HEADS):
        qo_ref[0, h] = q[:, h * HEAD_DIM:(h + 1) * HEAD_DIM].astype(BF16)
    v = kv[:, KV_WIDTH:].astype(F32)
    lane = lax.broadcasted_iota(I32, v.shape, 1)
    ones_col = jnp.where(lane == HEAD_DIM, 1.0, 0.0)
    for g in range(KV_HEADS):
        ko_ref[0, g] = k[:, g * HEAD_DIM:(g + 1) * HEAD_DIM].astype(BF16)
        vg = v if g == 0 else pltpu.roll(v, LANES - g * HEAD_DIM, 1)
        vo_ref[0, g] = jnp.where(lane < HEAD_DIM, vg, ones_col).astype(BF16)


def _qkprep_call(main, cos, sin, gq, gk, bd, tm, rope):
    b, t, _ = main.shape
    return pl.pallas_call(
        functools.partial(_qkprep_kernel, rope=rope),
        grid=(b, t // tm),
        in_specs=[pl.BlockSpec((1, tm, ATTN_WIDTH), lambda bi, i: (bi, i, COL_Q // ATTN_WIDTH)),
                  pl.BlockSpec((1, tm, 2 * KV_WIDTH), lambda bi, i: (bi, i, COL_KV // (2 * KV_WIDTH))),
                  pl.BlockSpec((tm, LANES), lambda bi, i: (i, 0)),
                  pl.BlockSpec((tm, LANES), lambda bi, i: (i, 0)),
                  pl.BlockSpec((1, ATTN_WIDTH), lambda bi, i: (0, 0)),
                  pl.BlockSpec((1, KV_WIDTH), lambda bi, i: (0, 0)),
                  pl.BlockSpec((ATTN_WIDTH, ATTN_WIDTH), lambda bi, i: (0, 0))],
        out_specs=[pl.BlockSpec((1, ATTN_HEADS, tm, HEAD_DIM), lambda bi, i: (bi, 0, i, 0)),
                   pl.BlockSpec((1, KV_HEADS, tm, HEAD_DIM), lambda bi, i: (bi, 0, i, 0)),
                   pl.BlockSpec((1, KV_HEADS, tm, LANES), lambda bi, i: (bi, 0, i, 0))],
        out_shape=[jax.ShapeDtypeStruct((b, ATTN_HEADS, t, HEAD_DIM), BF16),
                   jax.ShapeDtypeStruct((b, KV_HEADS, t, HEAD_DIM), BF16),
                   jax.ShapeDtypeStruct((b, KV_HEADS, t, LANES), BF16)],
        compiler_params=_cparams("parallel", "parallel"),
        name="qkprep",
    )(main, main, cos, sin, gq, gk, bd)


ATTN_BOUND_LIMIT = 40.0


def _attn_kernel(shift_ref, q_ref, *refs, bounded):
    o_ref = refs[-1]
    sources = list(zip(refs[:-1:2], refs[1:-1:2]))
    outs = []
    for r in range(ATTN_REP):
        q = q_ref[0, r]
        scores = [lax.dot_general(q, k_ref[0, 0], _NT, preferred_element_type=F32) for k_ref, _ in sources]
        if bounded:
            shift = shift_ref[0]
        else:
            shift = functools.reduce(jnp.maximum, [jnp.max(s, axis=-1, keepdims=True) for s in scores])
        acc = sum(_dot(jnp.exp(s - shift).astype(BF16), v_ref[0, 0]) for s, (_, v_ref) in zip(scores, sources))
        outs.append(acc[:, :HEAD_DIM] / acc[:, HEAD_DIM:HEAD_DIM + 1])
    o_ref[0] = jnp.concatenate(outs, axis=1).astype(o_ref.dtype)


def _attn_call(qh, kv_sources, bound, tq):
    b, _, t, _ = qh.shape
    in_specs = [pl.BlockSpec(memory_space=pltpu.SMEM),
                pl.BlockSpec((1, ATTN_REP, tq, HEAD_DIM), lambda bi, g, i: (bi, g, i, 0))]
    args = [bound.reshape(1), qh]
    for kh, vh in kv_sources:
        tk = kh.shape[2]
        in_specs += [pl.BlockSpec((1, 1, tk, HEAD_DIM), lambda bi, g, i: (bi, g, 0, 0)),
                     pl.BlockSpec((1, 1, tk, LANES), lambda bi, g, i: (bi, g, 0, 0))]
        args += [kh, vh]

    def call(bounded):
        return pl.pallas_call(
            functools.partial(_attn_kernel, bounded=bounded),
            grid=(b, KV_HEADS, t // tq),
            in_specs=in_specs,
            out_specs=pl.BlockSpec((1, tq, ATTN_REP * HEAD_DIM), lambda bi, g, i: (bi, i, g)),
            out_shape=jax.ShapeDtypeStruct((b, t, ATTN_WIDTH), BF16),
            compiler_params=_cparams("parallel", "parallel", "parallel"),
            name="attn_bounded" if bounded else "attn_rowmax",
        )(*args)

    return lax.cond(bound <= ATTN_BOUND_LIMIT, lambda: call(True), lambda: call(False))


def _conv3(u, prev_row, next_row, w):
    tm, s = u.shape[0], 8
    rid = lax.broadcasted_iota(I32, (s, u.shape[1]), 0)
    up = pltpu.roll(u, 1, 0)
    up = jnp.concatenate([jnp.where(rid == 0, prev_row, up[:s]), up[s:]], axis=0)
    dn = pltpu.roll(u, tm - 1, 0)
    dn = jnp.concatenate([dn[:tm - s], jnp.where(rid == s - 1, next_row, dn[tm - s:])], axis=0)
    return up * w[0:1, :] + u * w[1:2, :] + dn * w[2:3, :]


def _conv_kernel(x_ref, xp_ref, xn_ref, c_ref, cp_ref, cn_ref, s_ref, sp_ref, sn_ref, b_ref,
                 w_ref, bias_ref, wsc_ref, xo_ref, yo_ref):
    i, n = pl.program_id(1), pl.num_programs(1)
    keep_p = jnp.where(i > 0, 1.0, 0.0)
    keep_n = jnp.where(i < n - 1, 1.0, 0.0)
    last = BF16_ROWS - 1
    f = lambda r: r.astype(F32)
    u = f(x_ref[0])
    y = _conv3(u, f(xp_ref[0, last:last + 1, :]) * keep_p, f(xn_ref[0, 0:1, :]) * keep_n, w_ref[...]) + bias_ref[...]
    xo_ref[0] = _silu(y).astype(xo_ref.dtype)
    cs = f(c_ref[0]) * f(s_ref[0])
    cs_p = f(cp_ref[0, last:last + 1, :]) * f(sp_ref[0, last:last + 1, :]) * keep_p
    cs_n = f(cn_ref[0, 0:1, :]) * f(sn_ref[0, 0:1, :]) * keep_n
    yo_ref[0] = (f(b_ref[0]) * _conv3(cs, cs_p, cs_n, wsc_ref[...])).astype(yo_ref.dtype)


def _conv_call(main, conv_w, conv_b, sc_w, tm):
    b, t, _ = main.shape
    hb = tm // BF16_ROWS
    nh = t // BF16_ROWS

    def centre(width, col):
        return pl.BlockSpec((1, tm, width), lambda bi, i: (bi, i, col // width))

    def prev(width, col):
        return pl.BlockSpec((1, BF16_ROWS, width), lambda bi, i: (bi, jnp.maximum(i * hb - 1, 0), col // width))

    def nxt(width, col):
        return pl.BlockSpec((1, BF16_ROWS, width), lambda bi, i: (bi, jnp.minimum((i + 1) * hb, nh - 1), col // width))

    const = lambda shape: pl.BlockSpec(shape, lambda bi, i: (0, 0))
    return pl.pallas_call(
        _conv_kernel,
        grid=(b, t // tm),
        in_specs=[centre(SSD_CONV_DIM, COL_XBC), prev(SSD_CONV_DIM, COL_XBC), nxt(SSD_CONV_DIM, COL_XBC),
                  centre(SC_WIDTH, COL_SCC), prev(SC_WIDTH, COL_SCC), nxt(SC_WIDTH, COL_SCC),
                  centre(SC_WIDTH, COL_SCX), prev(SC_WIDTH, COL_SCX), nxt(SC_WIDTH, COL_SCX),
                  centre(SC_WIDTH, COL_SCB),
                  const((3, SSD_CONV_DIM)), const((1, SSD_CONV_DIM)), const((3, SC_WIDTH))],
        out_specs=[pl.BlockSpec((1, tm, SSD_CONV_DIM), lambda bi, i: (bi, i, 0)),
                   pl.BlockSpec((1, tm, SC_WIDTH), lambda bi, i: (bi, i, 0))],
        out_shape=[jax.ShapeDtypeStruct((b, t, SSD_CONV_DIM), BF16),
                   jax.ShapeDtypeStruct((b, t, SC_WIDTH), BF16)],
        compiler_params=_cparams("parallel", "parallel"),
        name="conv",
    )(main, main, main, main, main, main, main, main, main, main, conv_w, conv_b, sc_w)


def _ssd_kernel(xa_ref, dt_ref, dtT_ref, z_ref, s0_ref, brow_ref, arow_ref, bcol_ref, acol_ref, dexp_ref, nw_ref,
                y_ref, so_ref, yf_ref, st_ref):
    L = SSD_CHUNK
    t = xa_ref.shape[1]
    nc = t // L
    ri = lax.broadcasted_iota(I32, (L, L), 0)
    ci = lax.broadcasted_iota(I32, (L, L), 1)
    lane512 = lax.broadcasted_iota(I32, (L, SSD_INNER), 1)
    row512 = lax.broadcasted_iota(I32, (L, SSD_INNER), 0)
    blockmask = jnp.where((row512 >> 6) == (lane512 >> 8), 1.0, 0.0)
    lane128 = lax.broadcasted_iota(I32, (L, LANES), 1)
    a_row = -jnp.exp(arow_ref[...])
    a_col = -jnp.exp(acol_ref[...])

    st_ref[...] = s0_ref[0]

    def chunk(c, d):
        r0 = pl.multiple_of(c * L, L)
        causal = (ri >= ci) if d == 0 else (ri <= ci)
        tri = jnp.where(causal, 1.0, 0.0).astype(BF16)
        triT = jnp.where((ri <= ci) if d == 0 else (ri >= ci), 1.0, 0.0).astype(BF16)
        expand = jnp.where(row512 == (lane512 >> 6) + d * SSD_HEADS, 1.0, 0.0).astype(BF16)
        xa = xa_ref[0, pl.ds(r0, L), :]
        x = xa[:, :SSD_INNER]
        b_pair = xa[:, SSD_INNER:SSD_INNER + SSD_GN]
        c_pair = xa[:, SSD_INNER + SSD_GN:]
        dt_all = _softplus(dt_ref[0, pl.ds(r0, L), :] + brow_ref[...])
        a1, a2, a3 = _split3(dt_all * a_row)
        acs = _dot(tri, a1) + _dot(tri, a2) + _dot(tri, a3)
        dtT = _softplus(dtT_ref[0, :, pl.ds(r0, L)] + bcol_ref[...])
        acsT = _dot_exact_lhs(dtT * a_col, triT)
        edge = acs[L - 1:L, :] if d == 0 else acs[0:1, :]
        w_all = jnp.exp(edge - acs) * dt_all
        e_exp = jnp.exp(_dot_exact_lhs(acs, expand))
        w_exp = _dot(w_all.astype(BF16), expand)
        state = st_ref[d]
        y_off = _dot(c_pair, state.astype(BF16)) * e_exp
        gmats = [lax.dot_general(jnp.where((lane128 >> 6) == g, c_pair, jnp.zeros_like(c_pair)), b_pair, _NT,
                                 preferred_element_type=F32) for g in range(SSD_GROUPS)]
        pairs = []
        for pr in range(SSD_HEADS // 2):
            x_pair = x[:, pr * LANES:(pr + 1) * LANES]
            acc = jnp.zeros((L, LANES), F32)
            for hh in range(2):
                h = 2 * pr + hh
                gmat = gmats[h // (SSD_HEADS // SSD_GROUPS)]
                k = d * SSD_HEADS + h
                seg = acs[:, k:k + 1] - acsT[k:k + 1, :]
                m = jnp.exp(jnp.where(causal, seg, -jnp.inf)) * gmat * dtT[k:k + 1, :]
                xh = jnp.where((lane128 >> 6) == hh, x_pair, jnp.zeros_like(x_pair))
                acc = acc + _dot(m.astype(BF16), xh)
            pairs.append(acc)
        y = y_off + jnp.concatenate(pairs, axis=1)
        e_edge = e_exp[L - 1:L, :] if d == 0 else e_exp[0:1, :]
        ds = lax.dot_general(b_pair, (x.astype(F32) * w_exp).astype(BF16), _TN, preferred_element_type=F32)
        st_ref[d] = (state * e_edge + ds) * blockmask
        return y, x, r0

    def fwd(c, carry):
        y, _, r0 = chunk(c, 0)
        yf_ref[pl.ds(r0, L), :] = y
        return carry

    unroll = 4 if nc % 4 == 0 else (2 if nc % 2 == 0 else 1)
    lax.fori_loop(0, nc, fwd, 0, unroll=unroll)

    def bwd(i, carry):
        c = nc - 1 - i
        yb, x, r0 = chunk(c, 1)
        y = yf_ref[pl.ds(r0, L), :] + yb + dexp_ref[...] * x.astype(F32)
        gated = y * _silu(z_ref[0, pl.ds(r0, L), :].astype(F32))
        ms = jnp.mean(gated * gated, axis=-1, keepdims=True)
        y_ref[0, pl.ds(r0, L), :] = (gated * lax.rsqrt(ms + EPS) * nw_ref[...]).astype(y_ref.dtype)
        return carry

    lax.fori_loop(0, nc, bwd, 0, unroll=unroll)
    so_ref[0] = st_ref[...]


def _ssd_call(xa, dt, dtT, main, s0, brow, arow, bcol, acol, dexp, nw):
    b, t, _ = xa.shape
    c2 = lambda shape: pl.BlockSpec(shape, lambda bi: (0, 0))
    return pl.pallas_call(
        _ssd_kernel,
        grid=(b,),
        in_specs=[pl.BlockSpec((1, t, SSD_CONV_DIM), lambda bi: (bi, 0, 0)),
                  pl.BlockSpec((1, t, LANES), lambda bi: (bi, 0, 0)),
                  pl.BlockSpec((1, 2 * SSD_HEADS, t), lambda bi: (bi, 0, 0)),
                  pl.BlockSpec((1, t, SSD_INNER), lambda bi: (bi, 0, COL_Z // SSD_INNER)),
                  pl.BlockSpec((1, 2, SSD_GN, SSD_INNER), lambda bi: (bi, 0, 0, 0)),
                  c2((1, LANES)), c2((1, LANES)), c2((2 * SSD_HEADS, LANES)), c2((2 * SSD_HEADS, LANES)),
                  c2((1, SSD_INNER)), c2((1, SSD_INNER))],
        out_specs=[pl.BlockSpec((1, t, SSD_INNER), lambda bi: (bi, 0, 0)),
                   pl.BlockSpec((1, 2, SSD_GN, SSD_INNER), lambda bi: (bi, 0, 0, 0))],
        out_shape=[jax.ShapeDtypeStruct((b, t, SSD_INNER), BF16),
                   jax.ShapeDtypeStruct((b, 2, SSD_GN, SSD_INNER), F32)],
        scratch_shapes=[pltpu.VMEM((t, SSD_INNER), F32), pltpu.VMEM((2, SSD_GN, SSD_INNER), F32)],
        compiler_params=_cparams("parallel"),
        name="ssd",
    )(xa, dt, dtT, main, s0, brow, arow, bcol, acol, dexp, nw)


def _merge_kernel(ya_ref, ys_ref, yc_ref, g_ref, x_ref, m2_ref, m3_ref, m4_ref, wa_ref, ws_ref, wc_ref, wo_ref,
                  lg_ref, lb_ref, wr_ref, x1_ref, h2_ref, aff_ref, *, alpha):
    d = x_ref.shape[2]
    gate = lambda j: jax.nn.sigmoid(g_ref[0, :, j * d:(j + 1) * d].astype(F32))
    m = (gate(0) * _dot(ya_ref[0], wa_ref[...]) + gate(1) * _dot(ys_ref[0], ws_ref[...])
         + gate(2) * _dot(yc_ref[0], wc_ref[...]))
    out = _dot(m.astype(BF16), wo_ref[...])
    x1 = _layer_norm(alpha * x_ref[0] + m2_ref[0] * out) * lg_ref[...] + lb_ref[...]
    x1_ref[0] = x1
    h2 = _layer_norm(x1) * (1.0 + m4_ref[0]) + m3_ref[0]
    h2_ref[0] = h2.astype(h2_ref.dtype)
    logits = _dot_hp_nt(wr_ref[...], h2)
    e = jnp.exp(logits - jnp.max(logits, axis=0, keepdims=True))
    aff_ref[0] = e / jnp.sum(e, axis=0, keepdims=True)


def _merge_call(ya, ys, yc, main, x, m2, m3, m4, wa, ws, wc, wo, lg, lb, wrT, alpha, tm):
    b, t, d = x.shape
    tok = lambda w: pl.BlockSpec((1, tm, w), lambda bi, i: (bi, i, 0))
    per_b = pl.BlockSpec((1, 1, d), lambda bi, i: (bi, 0, 0))
    const = lambda shape: pl.BlockSpec(shape, lambda bi, i: (0, 0))
    return pl.pallas_call(
        functools.partial(_merge_kernel, alpha=alpha),
        grid=(b, t // tm),
        in_specs=[tok(ATTN_WIDTH), tok(SSD_INNER), tok(SC_WIDTH), tok(N_BRANCH * d), tok(d),
                  per_b, per_b, per_b,
                  const((ATTN_WIDTH, d)), const((SSD_INNER, d)), const((SC_WIDTH, d)), const((d, d)),
                  const((1, d)), const((1, d)), const((N_EXPERTS, d))],
        out_specs=[tok(d), tok(d), pl.BlockSpec((1, N_EXPERTS, tm), lambda bi, i: (bi, 0, i))],
        out_shape=[jax.ShapeDtypeStruct((b, t, d), F32), jax.ShapeDtypeStruct((b, t, d), BF16),
                   jax.ShapeDtypeStruct((b, N_EXPERTS, t), F32)],
        compiler_params=_cparams("parallel", "parallel"),
        name="merge",
    )(ya, ys, yc, main, x, m2, m3, m4, wa, ws, wc, wo, lg, lb, wrT)


def _route_kernel(aff_ref, pos_ref, w_ref, offs_ref, *, cap):
    aff = aff_ref[0]
    ne, t = aff.shape
    capf = float(cap)

    def step(i, lo):
        cand = lo | jnp.left_shift(jnp.int32(1), 30 - i)
        cnt = jnp.sum(jnp.where(aff >= pltpu.bitcast(cand, F32), 1.0, 0.0), axis=1, keepdims=True)
        return jnp.where(cnt >= capf, cand, lo)

    thr = pltpu.bitcast(lax.fori_loop(0, 31, step, jnp.zeros((ne, 1), I32)), F32)
    need = capf - jnp.sum(jnp.where(aff > thr, 1.0, 0.0), axis=1, keepdims=True)
    ri = lax.broadcasted_iota(I32, (LANES, LANES), 0)
    ci = lax.broadcasted_iota(I32, (LANES, LANES), 1)
    upper = jnp.where(ri < ci, 1.0, 0.0).astype(BF16)
    lane = lax.broadcasted_iota(I32, (ne, LANES), 1)
    carry_gt = jnp.zeros((ne, 1), F32)
    carry_eq = jnp.zeros((ne, 1), F32)
    offs = jnp.zeros((ne, LANES), F32)
    nblk = t // LANES
    for j in range(nblk):
        sl = slice(j * LANES, (j + 1) * LANES)
        aj = aff[:, sl]
        gj = jnp.where(aj > thr, 1.0, 0.0)
        ej = jnp.where(aj == thr, 1.0, 0.0)
        pre = _dot(jnp.concatenate([gj, ej], axis=0).astype(BF16), upper)
        pre_gt = pre[:ne] + carry_gt
        pre_eq = pre[ne:] + carry_eq
        sel = gj + ej * jnp.where(pre_eq < need, 1.0, 0.0)
        slot = pre_gt + jnp.minimum(pre_eq, need)
        pos_ref[0, :, sl] = jnp.where(sel > 0.5, slot, -1.0)
        w_ref[0, :, sl] = jnp.where(sel > 0.5, aj, 0.0)
        offs = jnp.where(lane == j, carry_gt + jnp.minimum(carry_eq, need), offs)
        carry_gt = carry_gt + jnp.sum(gj, axis=1, keepdims=True)
        carry_eq = carry_eq + jnp.sum(ej, axis=1, keepdims=True)
    offs = jnp.where(lane >= nblk, capf, offs)
    offs_ref[0] = offs.astype(I32)


def _route_call(affT, cap):
    b, ne, t = affT.shape
    spec = pl.BlockSpec((1, ne, t), lambda bi: (bi, 0, 0))
    return pl.pallas_call(
        functools.partial(_route_kernel, cap=cap),
        grid=(b,),
        in_specs=[spec],
        out_specs=[spec, spec, pl.BlockSpec((1, ne, LANES), lambda bi: (bi, 0, 0))],
        out_shape=[jax.ShapeDtypeStruct((b, ne, t), F32), jax.ShapeDtypeStruct((b, ne, t), F32),
                   jax.ShapeDtypeStruct((b, ne, LANES), I32)],
        compiler_params=_cparams("parallel"),
        name="route",
    )(affT)


EXPERT_GROUP = 4


def _slot_windows(offs_ref, bi, i, ne, per, cap, win):
    out = []
    for e in range(ne):
        lo = offs_ref[(bi * ne + e) * LANES + i * per]
        hi = offs_ref[(bi * ne + e) * LANES + (i + 1) * per]
        out.append((lo, hi, lo - lax.rem(lo, BF16_ROWS)))
    return out


def _window_start(wb, p, cap, win):
    return pl.multiple_of(jnp.minimum(wb + p * win, cap - win), BF16_ROWS)


def _gather_kernel(offs_ref, h_ref, pos_ref, *rest, win):
    xs_ref = rest[-1]
    bi, i = pl.program_id(0), pl.program_id(1)
    tm = h_ref.shape[1]
    ne, cap = xs_ref.shape[1], xs_ref.shape[2]
    windows = _slot_windows(offs_ref, bi, i, ne, tm // LANES, cap, win)
    h = h_ref[0]
    pos = pos_ref[0]
    row = lax.broadcasted_iota(I32, (win, tm), 0)

    @pl.when(i == 0)
    def _():
        xs_ref[...] = jnp.zeros_like(xs_ref)

    def onehot(e, start, first=None):
        slot = row + start
        if first is not None:
            slot = jnp.where(slot >= first, slot, -2)
        return jnp.where(pos[e:e + 1, :] == slot.astype(F32), 1.0, 0.0).astype(BF16)

    def add_window(e, start, rows):
        xs_ref[0, e, pl.ds(start, win), :] += rows.astype(xs_ref.dtype)

    for g0 in range(0, ne, EXPERT_GROUP):
        experts = range(g0, min(g0 + EXPERT_GROUP, ne))
        starts = [_window_start(windows[e][2], 0, cap, win) for e in experts]
        y = _dot(jnp.concatenate([onehot(e, s) for e, s in zip(experts, starts)], axis=0), h)
        for k, (e, s) in enumerate(zip(experts, starts)):
            add_window(e, s, y[k * win:(k + 1) * win])
    for p in range(1, pl.cdiv(cap, win)):
        for e in range(ne):
            lo, hi, wb = windows[e]

            @pl.when(hi > wb + p * win)
            def _():
                s = _window_start(wb, p, cap, win)
                add_window(e, s, _dot(onehot(e, s, first=wb + p * win), h))


def _gather_call(offs, h2, pos, cap, tm, slots_total, slot_base, buf=None):
    b, t, d = h2.shape
    ne = pos.shape[1]
    mean = tm * cap // t
    win = min(LANES, cap, pl.cdiv(mean + mean // 4 + BF16_ROWS, BF16_ROWS) * BF16_ROWS)
    in_specs = [pl.BlockSpec((1, tm, d), lambda bi, i, offs: (bi, i, 0)),
                pl.BlockSpec((1, ne, tm), lambda bi, i, offs: (bi, 0, i))]
    args = [offs, h2, pos]
    if buf is not None:
        in_specs.append(pl.BlockSpec(memory_space=pl.ANY))
        args.append(buf)
    return pl.pallas_call(
        functools.partial(_gather_kernel, win=win),
        grid_spec=pltpu.PrefetchScalarGridSpec(
            num_scalar_prefetch=1,
            grid=(b, t // tm),
            in_specs=in_specs,
            out_specs=pl.BlockSpec((1, ne, cap, d), lambda bi, i, offs: (bi, 0, slot_base // cap, 0))),
        out_shape=jax.ShapeDtypeStruct((b, ne, slots_total, d), BF16),
        input_output_aliases={} if buf is None else {3: 0},
        compiler_params=_cparams("parallel", "arbitrary"),
        name="gather",
    )(*args)


def _ffn_kernel(xs_ref, wg_ref, wu_ref, wd_ref, o_ref, acc_ref, *, group):
    f, nf = pl.program_id(1), pl.num_programs(1)
    b, _, slots, d = xs_ref.shape
    kc = 256

    def dot_w(a, w_ref):
        k = w_ref.shape[2]
        return sum(_dot(a[:, k0:k0 + kc], w_ref[0, 0, k0:k0 + kc, :].astype(BF16)) for k0 in range(0, k, kc))

    @pl.when(f == 0)
    def _():
        acc_ref[...] = jnp.zeros_like(acc_ref)

    for b0 in range(0, b, group):
        x = xs_ref[b0:b0 + group, 0].reshape(group * slots, d)
        hid = (_silu(dot_w(x, wg_ref)) * dot_w(x, wu_ref)).astype(BF16)
        acc_ref[b0:b0 + group] += dot_w(hid, wd_ref).reshape(group, slots, d)

    @pl.when(f == nf - 1)
    def _():
        o_ref[:, 0] = acc_ref[...].astype(o_ref.dtype)


def _ffn_call(xs, wg, wu, wd, layer, tf):
    b, ne, slots, d = xs.shape
    ff = wg.shape[3]
    tok = pl.BlockSpec((b, 1, slots, d), lambda e, f: (0, e, 0, 0))
    return pl.pallas_call(
        functools.partial(_ffn_kernel, group=1),
        grid=(ne, ff // tf),
        in_specs=[tok,
                  pl.BlockSpec((1, 1, d, tf), lambda e, f: (layer, e, 0, f)),
                  pl.BlockSpec((1, 1, d, tf), lambda e, f: (layer, e, 0, f)),
                  pl.BlockSpec((1, 1, tf, d), lambda e, f: (layer, e, f, 0))],
        out_specs=tok,
        out_shape=jax.ShapeDtypeStruct(xs.shape, BF16),
        scratch_shapes=[pltpu.VMEM((b, slots, d), F32)],
        compiler_params=_cparams("parallel", "arbitrary"),
        name="ffn",
    )(xs, wg, wu, wd)


def _combine_kernel(offs_ref, o_ref, pos_ref, w_ref, x_ref, m5_ref, lg_ref, lb_ref, y_ref, acc_ref, *, alpha, win):
    bi, i = pl.program_id(0), pl.program_id(1)
    tm = x_ref.shape[1]
    ne, cap = o_ref.shape[1], o_ref.shape[2]
    windows = _slot_windows(offs_ref, bi, i, ne, tm // LANES, cap, win)
    pos = pos_ref[0]
    w = w_ref[0]
    col = lax.broadcasted_iota(I32, (tm, win), 1)

    def onehot(e, start, first=None):
        slot = col + start
        if first is not None:
            slot = jnp.where(slot >= first, slot, -2)
        return jnp.where(pos[:, e:e + 1] == slot.astype(F32), w[:, e:e + 1], 0.0).astype(BF16)

    y = jnp.zeros((tm, o_ref.shape[3]), F32)
    for g0 in range(0, ne, EXPERT_GROUP):
        experts = range(g0, min(g0 + EXPERT_GROUP, ne))
        starts = [_window_start(windows[e][2], 0, cap, win) for e in experts]
        lhs = jnp.concatenate([onehot(e, s) for e, s in zip(experts, starts)], axis=1)
        rhs = jnp.concatenate([o_ref[0, e, pl.ds(s, win), :] for e, s in zip(experts, starts)], axis=0)
        y = y + _dot(lhs, rhs)
    acc_ref[...] = y
    for p in range(1, cap // win):
        for e in range(ne):
            lo, hi, wb = windows[e]

            @pl.when(hi > wb + p * win)
            def _():
                s = _window_start(wb, p, cap, win)
                acc_ref[...] += _dot(onehot(e, s, first=wb + p * win), o_ref[0, e, pl.ds(s, win), :])
    y_ref[0] = _layer_norm(alpha * x_ref[0] + m5_ref[0] * acc_ref[...]) * lg_ref[...] + lb_ref[...]


def _combine_call(offs, out, posT, wT, x1, m5, lg, lb, alpha, tm, cap, slot_base):
    b, t, d = x1.shape
    ne = out.shape[1]
    return pl.pallas_call(
        functools.partial(_combine_kernel, alpha=alpha, win=min(LANES, cap)),
        grid_spec=pltpu.PrefetchScalarGridSpec(
            num_scalar_prefetch=1,
            grid=(b, t // tm),
            in_specs=[pl.BlockSpec((1, ne, cap, d), lambda bi, i, offs: (bi, 0, slot_base // cap, 0)),
                      pl.BlockSpec((1, tm, ne), lambda bi, i, offs: (bi, i, 0)),
                      pl.BlockSpec((1, tm, ne), lambda bi, i, offs: (bi, i, 0)),
                      pl.BlockSpec((1, tm, d), lambda bi, i, offs: (bi, i, 0)),
                      pl.BlockSpec((1, 1, d), lambda bi, i, offs: (bi, 0, 0)),
                      pl.BlockSpec((1, d), lambda bi, i, offs: (0, 0)),
                      pl.BlockSpec((1, d), lambda bi, i, offs: (0, 0))],
            out_specs=pl.BlockSpec((1, tm, d), lambda bi, i, offs: (bi, i, 0)),
            scratch_shapes=[pltpu.VMEM((tm, d), F32)]),
        out_shape=jax.ShapeDtypeStruct((b, t, d), F32),
        compiler_params=_cparams("parallel", "arbitrary"),
        name="combine",
    )(offs, out, posT, wT, x1, m5, lg, lb)


def _rope_tables(n):
    rows = n // GRID_W
    row = jnp.repeat(jnp.arange(rows), GRID_W).astype(F32)
    col = jnp.tile(jnp.arange(GRID_W), rows).astype(F32)
    inv = ROPE_THETA ** (-jnp.arange(0, AXIS_ROT, 2, dtype=F32) / AXIS_ROT)
    ang = jnp.stack([row[:, None] * inv, col[:, None] * inv], axis=1)
    cos = jnp.repeat(jnp.cos(ang)[:, :, None, :], 2, axis=2).reshape(n, HEAD_DIM)
    sin = jnp.stack([-jnp.sin(ang), jnp.sin(ang)], axis=2).reshape(n, HEAD_DIM)
    return jnp.tile(cos, (1, LANES // HEAD_DIM)), jnp.tile(sin, (1, LANES // HEAD_DIM))


def _tile(t, pref):
    return pref if t % pref == 0 else t


def _moe(streams, wg, wu, wd, layer, lg, lb, alpha):
    caps = [CAPACITY_FACTOR * h2.shape[1] // N_EXPERTS for _, h2, _, _ in streams]
    bases = [sum(caps[:k]) for k in range(len(caps))]
    assert all(base % cap == 0 for base, cap in zip(bases, caps)) and sum(caps) % BF16_ROWS == 0
    xs, routed = None, []
    for (_, h2, affT, _), cap, base in zip(streams, caps, bases):
        t = h2.shape[1]
        pos, wsel, offs = _route_call(affT, cap)
        offs = offs.reshape(-1)
        xs = _gather_call(offs, h2, pos, cap, tm=_tile(t, 512), slots_total=sum(caps), slot_base=base, buf=xs)
        routed.append((offs, jnp.swapaxes(pos, 1, 2), jnp.swapaxes(wsel, 1, 2)))
    out = _ffn_call(xs, wg, wu, wd, layer, tf=_tile(wg.shape[3], 512))
    return [_combine_call(offs, out, posT, wT, x1, m5, lg, lb, alpha, tm=_tile(x1.shape[1], 512),
                          cap=cap, slot_base=base)
            for (x1, _, _, m5), (offs, posT, wT), cap, base in zip(streams, routed, caps, bases)]


def kernel(x, c, ctx, c_ctx, w_mod, b_mod, w_in, q_norm, k_norm, ssd_conv_w, ssd_conv_b, ssd_a_log, ssd_dt_bias,
           ssd_d, ssd_norm, sc_conv_w, w_br_attn, w_br_ssd, w_br_conv, w_o, ln1_g, ln1_b, w_router, w_exp_gate,
           w_exp_up, w_exp_down, ln2_g, ln2_b):
    b, n, d = x.shape
    nctx = ctx.shape[1]
    depth = w_mod.shape[0]
    alpha = (2 * depth) ** 0.25

    rows = -(-(b + 1) // 8) * 8
    cvec = jnp.concatenate([c, c_ctx[None], jnp.zeros((rows - b - 1, d), F32)], axis=0)
    mods = _mod_call(cvec, w_mod, b_mod)

    cos_l, sin_l = _rope_tables(n)
    cos_c, sin_c = jnp.ones((nctx, LANES), F32), jnp.zeros((nctx, LANES), F32)
    hid = jnp.arange(ATTN_WIDTH) // HEAD_DIM
    bd = (hid[:, None] == hid[None, :]).astype(BF16)
    dexp_all = jnp.repeat(ssd_d, SSD_HEAD_DIM, axis=1)
    pad16 = lambda v: jnp.pad(v.reshape(1, 2 * SSD_HEADS), ((0, 0), (0, LANES - 2 * SSD_HEADS)))
    col16 = lambda v: jnp.broadcast_to(v.reshape(2 * SSD_HEADS, 1), (2 * SSD_HEADS, LANES))

    w_in_t = jnp.swapaxes(w_in, 1, 2)
    w_main = w_in_t.astype(BF16)
    dt_lo, dt_hi = _REF_COLS["dt"]
    w_dt = jnp.pad(w_in_t[:, dt_lo:dt_hi, :], ((0, 0), (0, LANES - (dt_hi - dt_lo)), (0, 0)))
    x_ctx = ctx
    for i in range(depth):
        need_ctx = i < depth - 1
        gq = jnp.tile(q_norm[i], ATTN_HEADS)[None]
        gk = jnp.tile(k_norm[i], KV_HEADS)[None]
        brow, arow = pad16(ssd_dt_bias[i]), pad16(ssd_a_log[i])
        bcol, acol = col16(ssd_dt_bias[i]), col16(ssd_a_log[i])
        dexp, nw = dexp_all[i][None], ssd_norm[i][None]
        wa, ws, wc, wo = (w.astype(BF16) for w in (w_br_attn[i], w_br_ssd[i], w_br_conv[i], w_o[i]))
        wrT = w_router[i].T
        lg1, lb1, lg2, lb2 = ln1_g[i][None], ln1_b[i][None], ln2_g[i][None], ln2_b[i][None]
        mod_l = [m[:, None, :] for m in jnp.split(mods[i, :b], N_MOD, axis=-1)]
        mod_c = [jnp.broadcast_to(m[None, None, :], (b, 1, d)) for m in jnp.split(mods[i, b], N_MOD, axis=-1)]

        def mixer(xin, mod, cos, sin, rope, s0):
            t = xin.shape[1]
            main, dt, dtT = _inproj_call(xin, mod[0], mod[1], w_main, i, w_dt, tm=_tile(t, 512))
            qh, kh, vh = _qkprep_call(main, cos, sin, gq, gk, bd, tm=_tile(t, 512), rope=rope)
            xa, yc = _conv_call(main, ssd_conv_w[i], ssd_conv_b[i][None], sc_conv_w[i], tm=_tile(t, 512))
            ys, s_out = _ssd_call(xa, dt, dtT, main, s0, brow, arow, bcol, acol, dexp, nw)
            return main, qh, kh, vh, yc, ys, s_out

        def merge(xin, mod, main, ya, ys, yc):
            return _merge_call(ya, ys, yc, main, xin, mod[2], mod[3], mod[4], wa, ws, wc, wo, lg1, lb1, wrT,
                               alpha, tm=_tile(xin.shape[1], 512))

        zero_state = jnp.zeros((b, 2, SSD_GN, SSD_INNER), F32)
        main_c, qh_c, kh_c, vh_c, yc_c, ys_c, s_ctx = mixer(x_ctx, mod_c, cos_c, sin_c, False, zero_state)
        main_l, qh_l, kh_l, vh_l, yc_l, ys_l, _ = mixer(x, mod_l, cos_l, sin_l, True, s_ctx)
        bound = math.sqrt(HEAD_DIM) * jnp.max(jnp.abs(q_norm[i])) * jnp.max(jnp.abs(k_norm[i]))
        ya_l = _attn_call(qh_l, [(kh_c, vh_c), (kh_l, vh_l)], bound, tq=_tile(n, 256))
        streams = [(*merge(x, mod_l, main_l, ya_l, ys_l, yc_l), mod_l[5])]
        if need_ctx:
            ya_c = _attn_call(qh_c, [(kh_c, vh_c)], bound, tq=_tile(nctx, 256))
            streams.append((*merge(x_ctx, mod_c, main_c, ya_c, ys_c, yc_c), mod_c[5]))
        new = _moe(streams, w_exp_gate, w_exp_up, w_exp_down, i, lg2, lb2, alpha)
        x = new[0]
        if need_ctx:
            x_ctx = new[1]
    return x
```

```python
import functools
import math

import jax
import jax.numpy as jnp
from jax import lax
from jax.experimental import pallas as pl
from jax.experimental.pallas import tpu as pltpu

F32, BF16, I32 = jnp.float32, jnp.bfloat16, jnp.int32

HEAD_DIM = 64
ATTN_HEADS = 8
KV_HEADS = 2
ATTN_REP = ATTN_HEADS // KV_HEADS
ATTN_WIDTH = ATTN_HEADS * HEAD_DIM
KV_WIDTH = KV_HEADS * HEAD_DIM
AXIS_ROT = HEAD_DIM // 2
ROPE_THETA = 10000.0
GRID_W = 64
SSD_HEADS = 8
SSD_HEAD_DIM = 64
SSD_INNER = SSD_HEADS * SSD_HEAD_DIM
SSD_STATE = 64
SSD_GROUPS = 2
SSD_GN = SSD_GROUPS * SSD_STATE
SSD_CONV_DIM = SSD_INNER + 2 * SSD_GN
SSD_CHUNK = 128
SC_WIDTH = 512
N_BRANCH = 3
N_EXPERTS = 16
CAPACITY_FACTOR = 2
N_MOD = 6
EPS = 1e-6

LANES = 128
BF16_ROWS = 16
VMEM_LIMIT_BYTES = 56 * 1024 * 1024

COL_GATE, COL_XBC, COL_KV, COL_Q, COL_Z, COL_SCB, COL_SCC, COL_SCX = 0, 3072, 3840, 4096, 4608, 5120, 5632, 6144
MAIN_COLS = 6656

_NT = (((1,), (1,)), ((), ()))
_TN = (((0,), (0,)), ((), ()))


def _cparams(*sem):
    return pltpu.CompilerParams(dimension_semantics=sem, vmem_limit_bytes=VMEM_LIMIT_BYTES)


def _dot(a, b):
    return jnp.dot(a, b, preferred_element_type=F32)


def _split2(a):
    hi = a.astype(BF16)
    lo = (a - hi.astype(F32)).astype(BF16)
    return hi, lo


def _split3(a):
    p1 = a.astype(BF16)
    r1 = a - p1.astype(F32)
    p2 = r1.astype(BF16)
    p3 = (r1 - p2.astype(F32)).astype(BF16)
    return p1, p2, p3


def _dot_hp(a, w):
    ah, al = _split2(a)
    wh, wl = _split2(w)
    return _dot(ah, wh) + _dot(al, wh) + _dot(ah, wl)


def _dot_hp_nt(a, w):
    ah, al = _split2(a)
    wh, wl = _split2(w)
    d = lambda x, y: lax.dot_general(x, y, _NT, preferred_element_type=F32)
    return d(ah, wh) + d(al, wh) + d(ah, wl)


def _dot_exact_lhs(a, m):
    p1, p2, p3 = _split3(a)
    return _dot(p1, m) + _dot(p2, m) + _dot(p3, m)


def _layer_norm(x):
    mu = jnp.mean(x, axis=-1, keepdims=True)
    xc = x - mu
    var = jnp.mean(xc * xc, axis=-1, keepdims=True)
    return xc * lax.rsqrt(var + EPS)


def _softplus(x):
    return jnp.maximum(x, 0.0) + jnp.log(1.0 + jnp.exp(-jnp.abs(x)))


def _silu(x):
    return x * jax.nn.sigmoid(x)


def _mod_kernel(c_ref, w_ref, b_ref, o_ref):
    o_ref[0] = _dot_hp(_silu(c_ref[...]), w_ref[0]) + b_ref[0]


def _mod_call(cvec, w_mod, b_mod):
    depth, d, n = w_mod.shape
    rows = cvec.shape[0]
    tn = 1536
    return pl.pallas_call(
        _mod_kernel,
        grid=(depth, n // tn),
        in_specs=[pl.BlockSpec((rows, d), lambda l, j: (0, 0)),
                  pl.BlockSpec((1, d, tn), lambda l, j: (l, 0, j)),
                  pl.BlockSpec((1, 1, tn), lambda l, j: (l, 0, j))],
        out_specs=pl.BlockSpec((1, rows, tn), lambda l, j: (l, 0, j)),
        out_shape=jax.ShapeDtypeStruct((depth, rows, n), F32),
        compiler_params=_cparams("parallel", "parallel"),
        name="mod",
    )(cvec, w_mod, b_mod.reshape(depth, 1, n))


_REF_COLS = dict(k=(0, 128), v=(128, 256), xbc=(256, 1024), dt=(1024, 1040), q=(1040, 1552), z=(1552, 2064),
                 scb=(2064, 2576), scc=(2576, 3088), scx=(3088, 3600), gate=(3600, 6672))
_MAIN_ORDER = ("gate", "xbc", "k", "v", "q", "z", "scb", "scc", "scx")


def _inproj_kernel(x_ref, sh_ref, sc_ref, w_ref, wdt_ref, o_ref, dt_ref, dtT_ref, *, groups):
    h = _layer_norm(x_ref[0]) * (1.0 + sc_ref[0]) + sh_ref[0]
    hb = h.astype(BF16)
    step, c0 = 512, 0
    for name in _MAIN_ORDER:
        lo, hi = _REF_COLS[name]
        for r0 in range(lo, hi, step):
            r1 = min(r0 + step, hi)
            if name in groups:
                y = lax.dot_general(hb, w_ref[0, r0:r1, :], _NT, preferred_element_type=F32).astype(o_ref.dtype)
            else:
                y = jnp.zeros((hb.shape[0], r1 - r0), o_ref.dtype)
            o_ref[0, :, c0:c0 + r1 - r0] = y
            c0 += r1 - r0
    dt = _dot_hp_nt(h, wdt_ref[0])
    dt_ref[0] = dt
    dtT_ref[0] = dt.T[:2 * SSD_HEADS, :]


def _mod_spec(d, mrow, j, prefetch=False):
    if prefetch:
        return pl.BlockSpec((1, 1, d), lambda bi, i, _: (mrow(bi) + j, 0, 0))
    return pl.BlockSpec((1, 1, d), lambda bi, i: (mrow(bi) + j, 0, 0))


def _inproj_call(x, mods, mrow, w_t, layer, w_dt_t, tm, groups=_MAIN_ORDER):
    b, t, d = x.shape
    return pl.pallas_call(
        functools.partial(_inproj_kernel, groups=groups),
        grid=(b, t // tm),
        in_specs=[pl.BlockSpec((1, tm, d), lambda bi, i: (bi, i, 0)),
                  _mod_spec(d, mrow, 0), _mod_spec(d, mrow, 1),
                  pl.BlockSpec((1, w_t.shape[1], d), lambda bi, i: (layer, 0, 0)),
                  pl.BlockSpec((1, LANES, d), lambda bi, i: (layer, 0, 0))],
        out_specs=[pl.BlockSpec((1, tm, MAIN_COLS), lambda bi, i: (bi, i, 0)),
                   pl.BlockSpec((1, tm, LANES), lambda bi, i: (bi, i, 0)),
                   pl.BlockSpec((1, 2 * SSD_HEADS, tm), lambda bi, i: (bi, 0, i))],
        out_shape=[jax.ShapeDtypeStruct((b, t, MAIN_COLS), BF16),
                   jax.ShapeDtypeStruct((b, t, LANES), F32),
                   jax.ShapeDtypeStruct((b, 2 * SSD_HEADS, t), F32)],
        compiler_params=_cparams("parallel", "parallel"),
        name="inproj",
    )(x, mods, mods, w_t, w_dt_t)


def _head_rms(x, bd, g):
    hi, lo = _split2(x * x)
    ss = _dot(hi, bd) + _dot(lo, bd)
    return x * lax.rsqrt(ss * (1.0 / HEAD_DIM) + EPS) * g


def _rope(x, cos, sin_signed):
    half = AXIS_ROT // 2
    outs = []
    for j in range(x.shape[1] // LANES):
        xj = x[:, j * LANES:(j + 1) * LANES]
        lane = lax.broadcasted_iota(I32, xj.shape, 1)
        partner = jnp.where((lane & half) == 0, pltpu.roll(xj, LANES - half, 1), pltpu.roll(xj, half, 1))
        outs.append(xj * cos + partner * sin_signed)
    return outs[0] if len(outs) == 1 else jnp.concatenate(outs, axis=1)


def _qkprep_kernel(q_ref, kv_ref, cos_ref, sin_ref, gq_ref, gk_ref, bd_ref, qo_ref, ko_ref, vo_ref, *, rope):
    q = _head_rms(q_ref[0].astype(F32), bd_ref[...], gq_ref[...])
    kv = kv_ref[0]
    k = _head_rms(kv[:, :KV_WIDTH].astype(F32), bd_ref[:KV_WIDTH, :KV_WIDTH], gk_ref[...])
    if rope:
        q = _rope(q, cos_ref[...], sin_ref[...])
        k = _rope(k, cos_ref[...], sin_ref[...])
    q = q * (HEAD_DIM ** -0.5)
    for h in range(ATTN_HEADS):
        qo_ref[0, h] = q[:, h * HEAD_DIM:(h + 1) * HEAD_DIM].astype(BF16)
    v = kv[:, KV_WIDTH:].astype(F32)
    lane = lax.broadcasted_iota(I32, v.shape, 1)
    ones_col = jnp.where(lane == HEAD_DIM, 1.0, 0.0)
    for g in range(KV_HEADS):
        ko_ref[0, g] = k[:, g * HEAD_DIM:(g + 1) * HEAD_DIM].astype(BF16)
        vg = v if g == 0 else pltpu.roll(v, LANES - g * HEAD_DIM, 1)
        vo_ref[0, g] = jnp.where(lane < HEAD_DIM, vg, ones_col).astype(BF16)


def _qkprep_call(main, cos, sin, gq, gk, bd, tm, rope):
    b, t, _ = main.shape
    return pl.pallas_call(
        functools.partial(_qkprep_kernel, rope=rope),
        grid=(b, t // tm),
        in_specs=[pl.BlockSpec((1, tm, ATTN_WIDTH), lambda bi, i: (bi, i, COL_Q // ATTN_WIDTH)),
                  pl.BlockSpec((1, tm, 2 * KV_WIDTH), lambda bi, i: (bi, i, COL_KV // (2 * KV_WIDTH))),
                  pl.BlockSpec((tm, LANES), lambda bi, i: (i, 0)),
                  pl.BlockSpec((tm, LANES), lambda bi, i: (i, 0)),
                  pl.BlockSpec((1, ATTN_WIDTH), lambda bi, i: (0, 0)),
                  pl.BlockSpec((1, KV_WIDTH), lambda bi, i: (0, 0)),
                  pl.BlockSpec((ATTN_WIDTH, ATTN_WIDTH), lambda bi, i: (0, 0))],
        out_specs=[pl.BlockSpec((1, ATTN_HEADS, tm, HEAD_DIM), lambda bi, i: (bi, 0, i, 0)),
                   pl.BlockSpec((1, KV_HEADS, tm, HEAD_DIM), lambda bi, i: (bi, 0, i, 0)),
                   pl.BlockSpec((1, KV_HEADS, tm, LANES), lambda bi, i: (bi, 0, i, 0))],
        out_shape=[jax.ShapeDtypeStruct((b, ATTN_HEADS, t, HEAD_DIM), BF16),
                   jax.ShapeDtypeStruct((b, KV_HEADS, t, HEAD_DIM), BF16),
                   jax.ShapeDtypeStruct((b, KV_HEADS, t, LANES), BF16)],
        compiler_params=_cparams("parallel", "parallel"),
        name="qkprep",
    )(main, main, cos, sin, gq, gk, bd)


ATTN_BOUND_LIMIT = 40.0


def _attn_kernel(shift_ref, q_ref, *refs, bounded):
    o_ref = refs[-1]
    sources = list(zip(refs[:-1:2], refs[1:-1:2]))
    outs = []
    for r in range(ATTN_REP):
        q = q_ref[0, r]
        scores = [lax.dot_general(q, k_ref[0, 0], _NT, preferred_element_type=F32) for k_ref, _ in sources]
        if bounded:
            shift = shift_ref[0]
        else:
            shift = functools.reduce(jnp.maximum, [jnp.max(s, axis=-1, keepdims=True) for s in scores])
        acc = sum(_dot(jnp.exp(s - shift).astype(BF16), v_ref[0, 0]) for s, (_, v_ref) in zip(scores, sources))
        outs.append(acc[:, :HEAD_DIM] / acc[:, HEAD_DIM:HEAD_DIM + 1])
    o_ref[0] = jnp.concatenate(outs, axis=1).astype(o_ref.dtype)


def _attn_call(qh, kv_sources, bound, tq):
    b, _, t, _ = qh.shape
    in_specs = [pl.BlockSpec(memory_space=pltpu.SMEM),
                pl.BlockSpec((1, ATTN_REP, tq, HEAD_DIM), lambda bi, g, i: (bi, g, i, 0))]
    args = [bound.reshape(1), qh]
    for kh, vh in kv_sources:
        tk = kh.shape[2]
        in_specs += [pl.BlockSpec((1, 1, tk, HEAD_DIM), lambda bi, g, i: (bi, g, 0, 0)),
                     pl.BlockSpec((1, 1, tk, LANES), lambda bi, g, i: (bi, g, 0, 0))]
        args += [kh, vh]

    def call(bounded):
        return pl.pallas_call(
            functools.partial(_attn_kernel, bounded=bounded),
            grid=(b, KV_HEADS, t // tq),
            in_specs=in_specs,
            out_specs=pl.BlockSpec((1, tq, ATTN_REP * HEAD_DIM), lambda bi, g, i: (bi, i, g)),
            out_shape=jax.ShapeDtypeStruct((b, t, ATTN_WIDTH), BF16),
            compiler_params=_cparams("parallel", "parallel", "parallel"),
            name="attn_bounded" if bounded else "attn_rowmax",
        )(*args)

    return lax.cond(bound <= ATTN_BOUND_LIMIT, lambda: call(True), lambda: call(False))


def _conv3(u, prev_row, next_row, w):
    tm, s = u.shape[0], 8
    rid = lax.broadcasted_iota(I32, (s, u.shape[1]), 0)
    up = pltpu.roll(u, 1, 0)
    up = jnp.concatenate([jnp.where(rid == 0, prev_row, up[:s]), up[s:]], axis=0)
    dn = pltpu.roll(u, tm - 1, 0)
    dn = jnp.concatenate([dn[:tm - s], jnp.where(rid == s - 1, next_row, dn[tm - s:])], axis=0)
    return up * w[0:1, :] + u * w[1:2, :] + dn * w[2:3, :]


def _conv_kernel(x_ref, xp_ref, xn_ref, c_ref, cp_ref, cn_ref, s_ref, sp_ref, sn_ref, b_ref,
                 w_ref, bias_ref, wsc_ref, xo_ref, yo_ref):
    i, n = pl.program_id(1), pl.num_programs(1)
    keep_p = jnp.where(i > 0, 1.0, 0.0)
    keep_n = jnp.where(i < n - 1, 1.0, 0.0)
    last = BF16_ROWS - 1
    f = lambda r: r.astype(F32)
    u = f(x_ref[0])
    y = _conv3(u, f(xp_ref[0, last:last + 1, :]) * keep_p, f(xn_ref[0, 0:1, :]) * keep_n, w_ref[...]) + bias_ref[...]
    xo_ref[0] = _silu(y).astype(xo_ref.dtype)
    cs = f(c_ref[0]) * f(s_ref[0])
    cs_p = f(cp_ref[0, last:last + 1, :]) * f(sp_ref[0, last:last + 1, :]) * keep_p
    cs_n = f(cn_ref[0, 0:1, :]) * f(sn_ref[0, 0:1, :]) * keep_n
    yo_ref[0] = (f(b_ref[0]) * _conv3(cs, cs_p, cs_n, wsc_ref[...])).astype(yo_ref.dtype)


def _conv_call(main, conv_w, conv_b, sc_w, tm):
    b, t, _ = main.shape
    hb = tm // BF16_ROWS
    nh = t // BF16_ROWS

    def centre(width, col):
        return pl.BlockSpec((1, tm, width), lambda bi, i: (bi, i, col // width))

    def prev(width, col):
        return pl.BlockSpec((1, BF16_ROWS, width), lambda bi, i: (bi, jnp.maximum(i * hb - 1, 0), col // width))

    def nxt(width, col):
        return pl.BlockSpec((1, BF16_ROWS, width), lambda bi, i: (bi, jnp.minimum((i + 1) * hb, nh - 1), col // width))

    const = lambda shape: pl.BlockSpec(shape, lambda bi, i: (0, 0))
    return pl.pallas_call(
        _conv_kernel,
        grid=(b, t // tm),
        in_specs=[centre(SSD_CONV_DIM, COL_XBC), prev(SSD_CONV_DIM, COL_XBC), nxt(SSD_CONV_DIM, COL_XBC),
                  centre(SC_WIDTH, COL_SCC), prev(SC_WIDTH, COL_SCC), nxt(SC_WIDTH, COL_SCC),
                  centre(SC_WIDTH, COL_SCX), prev(SC_WIDTH, COL_SCX), nxt(SC_WIDTH, COL_SCX),
                  centre(SC_WIDTH, COL_SCB),
                  const((3, SSD_CONV_DIM)), const((1, SSD_CONV_DIM)), const((3, SC_WIDTH))],
        out_specs=[pl.BlockSpec((1, tm, SSD_CONV_DIM), lambda bi, i: (bi, i, 0)),
                   pl.BlockSpec((1, tm, SC_WIDTH), lambda bi, i: (bi, i, 0))],
        out_shape=[jax.ShapeDtypeStruct((b, t, SSD_CONV_DIM), BF16),
                   jax.ShapeDtypeStruct((b, t, SC_WIDTH), BF16)],
        compiler_params=_cparams("parallel", "parallel"),
        name="conv",
    )(main, main, main, main, main, main, main, main, main, main, conv_w, conv_b, sc_w)


def _ssd_kernel(xa_ref, dt_ref, dtT_ref, z_ref, s0_ref, brow_ref, arow_ref, bcol_ref, acol_ref, dexp_ref, nw_ref,
                y_ref, so_ref, yf_ref, st_ref):
    L = SSD_CHUNK
    t = xa_ref.shape[1]
    nc = t // L
    ri = lax.broadcasted_iota(I32, (L, L), 0)
    ci = lax.broadcasted_iota(I32, (L, L), 1)
    lane512 = lax.broadcasted_iota(I32, (L, SSD_INNER), 1)
    row512 = lax.broadcasted_iota(I32, (L, SSD_INNER), 0)
    blockmask = jnp.where((row512 >> 6) == (lane512 >> 8), 1.0, 0.0)
    lane128 = lax.broadcasted_iota(I32, (L, LANES), 1)
    a_row = -jnp.exp(arow_ref[...])
    a_col = -jnp.exp(acol_ref[...])

    st_ref[...] = s0_ref[0]

    def chunk(c, d):
        r0 = pl.multiple_of(c * L, L)
        causal = (ri >= ci) if d == 0 else (ri <= ci)
        tri = jnp.where(causal, 1.0, 0.0).astype(BF16)
        triT = jnp.where((ri <= ci) if d == 0 else (ri >= ci), 1.0, 0.0).astype(BF16)
        expand = jnp.where(row512 == (lane512 >> 6) + d * SSD_HEADS, 1.0, 0.0).astype(BF16)
        xa = xa_ref[0, pl.ds(r0, L), :]
        x = xa[:, :SSD_INNER]
        b_pair = xa[:, SSD_INNER:SSD_INNER + SSD_GN]
        c_pair = xa[:, SSD_INNER + SSD_GN:]
        dt_all = _softplus(dt_ref[0, pl.ds(r0, L), :] + brow_ref[...])
        a1, a2, a3 = _split3(dt_all * a_row)
        acs = _dot(tri, a1) + _dot(tri, a2) + _dot(tri, a3)
        dtT = _softplus(dtT_ref[0, :, pl.ds(r0, L)] + bcol_ref[...])
        acsT = _dot_exact_lhs(dtT * a_col, triT)
        edge = acs[L - 1:L, :] if d == 0 else acs[0:1, :]
        w_all = jnp.exp(edge - acs) * dt_all
        e_exp = jnp.exp(_dot_exact_lhs(acs, expand))
        w_exp = _dot(w_all.astype(BF16), expand)
        state = st_ref[d]
        y_off = _dot(c_pair, state.astype(BF16)) * e_exp
        gmats = [lax.dot_general(jnp.where((lane128 >> 6) == g, c_pair, jnp.zeros_like(c_pair)), b_pair, _NT,
                                 preferred_element_type=F32) for g in range(SSD_GROUPS)]
        pairs = []
        for pr in range(SSD_HEADS // 2):
            x_pair = x[:, pr * LANES:(pr + 1) * LANES]
            acc = jnp.zeros((L, LANES), F32)
            for hh in range(2):
                h = 2 * pr + hh
                gmat = gmats[h // (SSD_HEADS // SSD_GROUPS)]
                k = d * SSD_HEADS + h
                seg = acs[:, k:k + 1] - acsT[k:k + 1, :]
                m = jnp.exp(jnp.where(causal, seg, -jnp.inf)) * gmat * dtT[k:k + 1, :]
                xh = jnp.where((lane128 >> 6) == hh, x_pair, jnp.zeros_like(x_pair))
                acc = acc + _dot(m.astype(BF16), xh)
            pairs.append(acc)
        y = y_off + jnp.concatenate(pairs, axis=1)
        e_edge = e_exp[L - 1:L, :] if d == 0 else e_exp[0:1, :]
        ds = lax.dot_general(b_pair, (x.astype(F32) * w_exp).astype(BF16), _TN, preferred_element_type=F32)
        st_ref[d] = (state * e_edge + ds) * blockmask
        return y, x, r0

    def fwd(c, carry):
        y, _, r0 = chunk(c, 0)
        yf_ref[pl.ds(r0, L), :] = y
        return carry

    unroll = 4 if nc % 4 == 0 else (2 if nc % 2 == 0 else 1)
    lax.fori_loop(0, nc, fwd, 0, unroll=unroll)

    def bwd(i, carry):
        c = nc - 1 - i
        yb, x, r0 = chunk(c, 1)
        y = yf_ref[pl.ds(r0, L), :] + yb + dexp_ref[...] * x.astype(F32)
        gated = y * _silu(z_ref[0, pl.ds(r0, L), :].astype(F32))
        ms = jnp.mean(gated * gated, axis=-1, keepdims=True)
        y_ref[0, pl.ds(r0, L), :] = (gated * lax.rsqrt(ms + EPS) * nw_ref[...]).astype(y_ref.dtype)
        return carry

    lax.fori_loop(0, nc, bwd, 0, unroll=unroll)
    so_ref[0] = st_ref[...]


def _ssd_call(xa, dt, dtT, main, s0, brow, arow, bcol, acol, dexp, nw):
    b, t, _ = xa.shape
    c2 = lambda shape: pl.BlockSpec(shape, lambda bi: (0, 0))
    return pl.pallas_call(
        _ssd_kernel,
        grid=(b,),
        in_specs=[pl.BlockSpec((1, t, SSD_CONV_DIM), lambda bi: (bi, 0, 0)),
                  pl.BlockSpec((1, t, LANES), lambda bi: (bi, 0, 0)),
                  pl.BlockSpec((1, 2 * SSD_HEADS, t), lambda bi: (bi, 0, 0)),
                  pl.BlockSpec((1, t, SSD_INNER), lambda bi: (bi, 0, COL_Z // SSD_INNER)),
                  pl.BlockSpec((1, 2, SSD_GN, SSD_INNER), lambda bi: (bi, 0, 0, 0)),
                  c2((1, LANES)), c2((1, LANES)), c2((2 * SSD_HEADS, LANES)), c2((2 * SSD_HEADS, LANES)),
                  c2((1, SSD_INNER)), c2((1, SSD_INNER))],
        out_specs=[pl.BlockSpec((1, t, SSD_INNER), lambda bi: (bi, 0, 0)),
                   pl.BlockSpec((1, 2, SSD_GN, SSD_INNER), lambda bi: (bi, 0, 0, 0))],
        out_shape=[jax.ShapeDtypeStruct((b, t, SSD_INNER), BF16),
                   jax.ShapeDtypeStruct((b, 2, SSD_GN, SSD_INNER), F32)],
        scratch_shapes=[pltpu.VMEM((t, SSD_INNER), F32), pltpu.VMEM((2, SSD_GN, SSD_INNER), F32)],
        compiler_params=_cparams("parallel"),
        name="ssd",
    )(xa, dt, dtT, main, s0, brow, arow, bcol, acol, dexp, nw)


def _merge_kernel(ya_ref, ys_ref, yc_ref, g_ref, x_ref, m2_ref, m3_ref, m4_ref, wa_ref, ws_ref, wc_ref, wo_ref,
                  lg_ref, lb_ref, wr_ref, x1_ref, h2_ref, aff_ref, *, alpha):
    d = x_ref.shape[2]
    gate = lambda j: jax.nn.sigmoid(g_ref[0, :, j * d:(j + 1) * d].astype(F32))
    m = (gate(0) * _dot(ya_ref[0], wa_ref[...]) + gate(1) * _dot(ys_ref[0], ws_ref[...])
         + gate(2) * _dot(yc_ref[0], wc_ref[...]))
    out = _dot(m.astype(BF16), wo_ref[...])
    x1 = _layer_norm(alpha * x_ref[0] + m2_ref[0] * out) * lg_ref[...] + lb_ref[...]
    x1_ref[0] = x1
    h2 = _layer_norm(x1) * (1.0 + m4_ref[0]) + m3_ref[0]
    h2_ref[0] = h2.astype(h2_ref.dtype)
    logits = _dot_hp_nt(wr_ref[...], h2)
    e = jnp.exp(logits - jnp.max(logits, axis=0, keepdims=True))
    aff_ref[0] = e / jnp.sum(e, axis=0, keepdims=True)


def _merge_call(ya, ys, yc, main, x, mods, mrow, wa, ws, wc, wo, lg, lb, wrT, alpha, tm):
    b, t, d = x.shape
    tok = lambda w: pl.BlockSpec((1, tm, w), lambda bi, i: (bi, i, 0))
    const = lambda shape: pl.BlockSpec(shape, lambda bi, i: (0, 0))
    return pl.pallas_call(
        functools.partial(_merge_kernel, alpha=alpha),
        grid=(b, t // tm),
        in_specs=[tok(ATTN_WIDTH), tok(SSD_INNER), tok(SC_WIDTH), tok(N_BRANCH * d), tok(d),
                  _mod_spec(d, mrow, 2), _mod_spec(d, mrow, 3), _mod_spec(d, mrow, 4),
                  const((ATTN_WIDTH, d)), const((SSD_INNER, d)), const((SC_WIDTH, d)), const((d, d)),
                  const((1, d)), const((1, d)), const((N_EXPERTS, d))],
        out_specs=[tok(d), tok(d), pl.BlockSpec((1, N_EXPERTS, tm), lambda bi, i: (bi, 0, i))],
        out_shape=[jax.ShapeDtypeStruct((b, t, d), F32), jax.ShapeDtypeStruct((b, t, d), BF16),
                   jax.ShapeDtypeStruct((b, N_EXPERTS, t), F32)],
        compiler_params=_cparams("parallel", "parallel"),
        name="merge",
    )(ya, ys, yc, main, x, mods, mods, mods, wa, ws, wc, wo, lg, lb, wrT)


def _route_kernel(aff_ref, pos_ref, w_ref, offs_ref, *, cap):
    aff = aff_ref[0]
    ne, t = aff.shape
    capf = float(cap)

    def enough(cand):
        return jnp.sum(jnp.where(aff >= pltpu.bitcast(cand, F32), 1.0, 0.0), axis=1, keepdims=True) >= capf

    def step(i, lo):
        b0 = jnp.left_shift(jnp.int32(1), 28 - 2 * i)
        b1 = b0 + b0
        c01, c10, c11 = lo | b0, lo | b1, lo | b1 | b0
        return jnp.where(enough(c11), c11, jnp.where(enough(c10), c10, jnp.where(enough(c01), c01, lo)))

    lo = jnp.zeros((ne, 1), I32)
    top = lo | jnp.int32(1 << 30)
    lo = jnp.where(enough(top), top, lo)
    thr = pltpu.bitcast(lax.fori_loop(0, 15, step, lo), F32)
    need = capf - jnp.sum(jnp.where(aff > thr, 1.0, 0.0), axis=1, keepdims=True)
    ri = lax.broadcasted_iota(I32, (LANES, LANES), 0)
    ci = lax.broadcasted_iota(I32, (LANES, LANES), 1)
    upper = jnp.where(ri < ci, 1.0, 0.0).astype(BF16)
    lane = lax.broadcasted_iota(I32, (ne, LANES), 1)
    carry_gt = jnp.zeros((ne, 1), F32)
    carry_eq = jnp.zeros((ne, 1), F32)
    offs = jnp.zeros((ne, LANES), F32)
    nblk = t // LANES
    for j in range(nblk):
        sl = slice(j * LANES, (j + 1) * LANES)
        aj = aff[:, sl]
        gj = jnp.where(aj > thr, 1.0, 0.0)
        ej = jnp.where(aj == thr, 1.0, 0.0)
        pre = _dot(jnp.concatenate([gj, ej], axis=0).astype(BF16), upper)
        pre_gt = pre[:ne] + carry_gt
        pre_eq = pre[ne:] + carry_eq
        sel = gj + ej * jnp.where(pre_eq < need, 1.0, 0.0)
        slot = pre_gt + jnp.minimum(pre_eq, need)
        pos_ref[0, :, sl] = jnp.where(sel > 0.5, slot, -1.0)
        w_ref[0, :, sl] = jnp.where(sel > 0.5, aj, 0.0)
        offs = jnp.where(lane == j, carry_gt + jnp.minimum(carry_eq, need), offs)
        carry_gt = carry_gt + jnp.sum(gj, axis=1, keepdims=True)
        carry_eq = carry_eq + jnp.sum(ej, axis=1, keepdims=True)
    offs = jnp.where(lane >= nblk, capf, offs)
    offs_ref[0] = offs.astype(I32)


def _route_call(affT, cap):
    b, ne, t = affT.shape
    spec = pl.BlockSpec((1, ne, t), lambda bi: (bi, 0, 0))
    return pl.pallas_call(
        functools.partial(_route_kernel, cap=cap),
        grid=(b,),
        in_specs=[spec],
        out_specs=[spec, spec, pl.BlockSpec((1, ne, LANES), lambda bi: (bi, 0, 0))],
        out_shape=[jax.ShapeDtypeStruct((b, ne, t), F32), jax.ShapeDtypeStruct((b, ne, t), F32),
                   jax.ShapeDtypeStruct((b, ne, LANES), I32)],
        compiler_params=_cparams("parallel"),
        name="route",
    )(affT)


EXPERT_GROUP = 4


def _slot_windows(offs_ref, bi, i, ne, per, cap, win):
    out = []
    for e in range(ne):
        lo = offs_ref[(bi * ne + e) * LANES + i * per]
        hi = offs_ref[(bi * ne + e) * LANES + (i + 1) * per]
        out.append((lo, hi, lo - lax.rem(lo, BF16_ROWS)))
    return out


def _window_start(wb, p, cap, win):
    return pl.multiple_of(jnp.minimum(wb + p * win, cap - win), BF16_ROWS)


def _gather_kernel(offs_ref, h_ref, pos_ref, *rest, win):
    xs_ref = rest[-1]
    bi, i = pl.program_id(0), pl.program_id(1)
    tm = h_ref.shape[1]
    ne, cap = xs_ref.shape[1], xs_ref.shape[2]
    windows = _slot_windows(offs_ref, bi, i, ne, tm // LANES, cap, win)
    h = h_ref[0]
    pos = pos_ref[0]
    row = lax.broadcasted_iota(I32, (win, tm), 0)

    @pl.when(i == 0)
    def _():
        xs_ref[...] = jnp.zeros_like(xs_ref)

    def onehot(e, start, first=None):
        slot = row + start
        if first is not None:
            slot = jnp.where(slot >= first, slot, -2)
        return jnp.where(pos[e:e + 1, :] == slot.astype(F32), 1.0, 0.0).astype(BF16)

    def add_window(e, start, rows):
        xs_ref[0, e, pl.ds(start, win), :] += rows.astype(xs_ref.dtype)

    for g0 in range(0, ne, EXPERT_GROUP):
        experts = range(g0, min(g0 + EXPERT_GROUP, ne))
        starts = [_window_start(windows[e][2], 0, cap, win) for e in experts]
        y = _dot(jnp.concatenate([onehot(e, s) for e, s in zip(experts, starts)], axis=0), h)
        for k, (e, s) in enumerate(zip(experts, starts)):
            add_window(e, s, y[k * win:(k + 1) * win])
    for p in range(1, pl.cdiv(cap, win)):
        for e in range(ne):
            lo, hi, wb = windows[e]

            @pl.when(hi > wb + p * win)
            def _():
                s = _window_start(wb, p, cap, win)
                add_window(e, s, _dot(onehot(e, s, first=wb + p * win), h))


def _gather_call(offs, h2, pos, cap, tm, slots_total, slot_base, buf=None):
    b, t, d = h2.shape
    ne = pos.shape[1]
    mean = tm * cap // t
    win = min(LANES, cap, pl.cdiv(mean + mean // 4 + BF16_ROWS, BF16_ROWS) * BF16_ROWS)
    in_specs = [pl.BlockSpec((1, tm, d), lambda bi, i, offs: (bi, i, 0)),
                pl.BlockSpec((1, ne, tm), lambda bi, i, offs: (bi, 0, i))]
    args = [offs, h2, pos]
    if buf is not None:
        in_specs.append(pl.BlockSpec(memory_space=pl.ANY))
        args.append(buf)
    return pl.pallas_call(
        functools.partial(_gather_kernel, win=win),
        grid_spec=pltpu.PrefetchScalarGridSpec(
            num_scalar_prefetch=1,
            grid=(b, t // tm),
            in_specs=in_specs,
            out_specs=pl.BlockSpec((1, ne, cap, d), lambda bi, i, offs: (bi, 0, slot_base // cap, 0))),
        out_shape=jax.ShapeDtypeStruct((b, ne, slots_total, d), BF16),
        input_output_aliases={} if buf is None else {3: 0},
        compiler_params=_cparams("parallel", "arbitrary"),
        name="gather",
    )(*args)


def _ffn_kernel(xs_ref, wg_ref, wu_ref, wd_ref, o_ref, acc_ref, *, group):
    f, nf = pl.program_id(1), pl.num_programs(1)
    b, _, slots, d = xs_ref.shape
    kc = 256

    def dot_w(a, w_ref):
        k = w_ref.shape[2]
        return sum(_dot(a[:, k0:k0 + kc], w_ref[0, 0, k0:k0 + kc, :].astype(BF16)) for k0 in range(0, k, kc))

    @pl.when(f == 0)
    def _():
        acc_ref[...] = jnp.zeros_like(acc_ref)

    for b0 in range(0, b, group):
        x = xs_ref[b0:b0 + group, 0].reshape(group * slots, d)
        hid = (_silu(dot_w(x, wg_ref)) * dot_w(x, wu_ref)).astype(BF16)
        acc_ref[b0:b0 + group] += dot_w(hid, wd_ref).reshape(group, slots, d)

    @pl.when(f == nf - 1)
    def _():
        o_ref[:, 0] = acc_ref[...].astype(o_ref.dtype)


def _ffn_call(xs, wg, wu, wd, layer, tf):
    b, ne, slots, d = xs.shape
    ff = wg.shape[3]
    tok = pl.BlockSpec((b, 1, slots, d), lambda e, f: (0, e, 0, 0))
    return pl.pallas_call(
        functools.partial(_ffn_kernel, group=1),
        grid=(ne, ff // tf),
        in_specs=[tok,
                  pl.BlockSpec((1, 1, d, tf), lambda e, f: (layer, e, 0, f)),
                  pl.BlockSpec((1, 1, d, tf), lambda e, f: (layer, e, 0, f)),
                  pl.BlockSpec((1, 1, tf, d), lambda e, f: (layer, e, f, 0))],
        out_specs=tok,
        out_shape=jax.ShapeDtypeStruct(xs.shape, BF16),
        scratch_shapes=[pltpu.VMEM((b, slots, d), F32)],
        compiler_params=_cparams("parallel", "arbitrary"),
        name="ffn",
    )(xs, wg, wu, wd)


def _combine_kernel(offs_ref, o_ref, pos_ref, w_ref, x_ref, m5_ref, lg_ref, lb_ref, y_ref, acc_ref, *, alpha, win):
    bi, i = pl.program_id(0), pl.program_id(1)
    tm = x_ref.shape[1]
    ne, cap = o_ref.shape[1], o_ref.shape[2]
    windows = _slot_windows(offs_ref, bi, i, ne, tm // LANES, cap, win)
    pos = pos_ref[0]
    w = w_ref[0]
    col = lax.broadcasted_iota(I32, (tm, win), 1)

    def onehot(e, start, first=None):
        slot = col + start
        if first is not None:
            slot = jnp.where(slot >= first, slot, -2)
        return jnp.where(pos[:, e:e + 1] == slot.astype(F32), w[:, e:e + 1], 0.0).astype(BF16)

    y = jnp.zeros((tm, o_ref.shape[3]), F32)
    for g0 in range(0, ne, EXPERT_GROUP):
        experts = range(g0, min(g0 + EXPERT_GROUP, ne))
        starts = [_window_start(windows[e][2], 0, cap, win) for e in experts]
        lhs = jnp.concatenate([onehot(e, s) for e, s in zip(experts, starts)], axis=1)
        rhs = jnp.concatenate([o_ref[0, e, pl.ds(s, win), :] for e, s in zip(experts, starts)], axis=0)
        y = y + _dot(lhs, rhs)
    acc_ref[...] = y
    for p in range(1, cap // win):
        for e in range(ne):
            lo, hi, wb = windows[e]

            @pl.when(hi > wb + p * win)
            def _():
                s = _window_start(wb, p, cap, win)
                acc_ref[...] += _dot(onehot(e, s, first=wb + p * win), o_ref[0, e, pl.ds(s, win), :])
    y_ref[0] = _layer_norm(alpha * x_ref[0] + m5_ref[0] * acc_ref[...]) * lg_ref[...] + lb_ref[...]


def _combine_call(offs, out, posT, wT, x1, mods, mrow, lg, lb, alpha, tm, cap, slot_base):
    b, t, d = x1.shape
    ne = out.shape[1]
    return pl.pallas_call(
        functools.partial(_combine_kernel, alpha=alpha, win=min(LANES, cap)),
        grid_spec=pltpu.PrefetchScalarGridSpec(
            num_scalar_prefetch=1,
            grid=(b, t // tm),
            in_specs=[pl.BlockSpec((1, ne, cap, d), lambda bi, i, offs: (bi, 0, slot_base // cap, 0)),
                      pl.BlockSpec((1, tm, ne), lambda bi, i, offs: (bi, i, 0)),
                      pl.BlockSpec((1, tm, ne), lambda bi, i, offs: (bi, i, 0)),
                      pl.BlockSpec((1, tm, d), lambda bi, i, offs: (bi, i, 0)),
                      _mod_spec(d, mrow, 5, prefetch=True),
                      pl.BlockSpec((1, d), lambda bi, i, offs: (0, 0)),
                      pl.BlockSpec((1, d), lambda bi, i, offs: (0, 0))],
            out_specs=pl.BlockSpec((1, tm, d), lambda bi, i, offs: (bi, i, 0)),
            scratch_shapes=[pltpu.VMEM((tm, d), F32)]),
        out_shape=jax.ShapeDtypeStruct((b, t, d), F32),
        compiler_params=_cparams("parallel", "arbitrary"),
        name="combine",
    )(offs, out, posT, wT, x1, mods, lg, lb)


def _rope_tables(n):
    rows = n // GRID_W
    row = jnp.repeat(jnp.arange(rows), GRID_W).astype(F32)
    col = jnp.tile(jnp.arange(GRID_W), rows).astype(F32)
    inv = ROPE_THETA ** (-jnp.arange(0, AXIS_ROT, 2, dtype=F32) / AXIS_ROT)
    ang = jnp.stack([row[:, None] * inv, col[:, None] * inv], axis=1)
    cos = jnp.repeat(jnp.cos(ang)[:, :, None, :], 2, axis=2).reshape(n, HEAD_DIM)
    sin = jnp.stack([-jnp.sin(ang), jnp.sin(ang)], axis=2).reshape(n, HEAD_DIM)
    return jnp.tile(cos, (1, LANES // HEAD_DIM)), jnp.tile(sin, (1, LANES // HEAD_DIM))


def _tile(t, pref):
    return pref if t % pref == 0 else t


def _moe(streams, mods, wg, wu, wd, layer, lg, lb, alpha):
    caps = [CAPACITY_FACTOR * h2.shape[1] // N_EXPERTS for _, h2, _, _ in streams]
    bases = [sum(caps[:k]) for k in range(len(caps))]
    assert all(base % cap == 0 for base, cap in zip(bases, caps)) and sum(caps) % BF16_ROWS == 0
    xs, routed = None, []
    for (_, h2, affT, _), cap, base in zip(streams, caps, bases):
        t = h2.shape[1]
        pos, wsel, offs = _route_call(affT, cap)
        offs = offs.reshape(-1)
        xs = _gather_call(offs, h2, pos, cap, tm=_tile(t, 512), slots_total=sum(caps), slot_base=base, buf=xs)
        routed.append((offs, jnp.swapaxes(pos, 1, 2), jnp.swapaxes(wsel, 1, 2)))
    out = _ffn_call(xs, wg, wu, wd, layer, tf=_tile(wg.shape[3], 512))
    return [_combine_call(offs, out, posT, wT, x1, mods, mrow, lg, lb, alpha, tm=_tile(x1.shape[1], 512),
                          cap=cap, slot_base=base)
            for (x1, _, _, mrow), (offs, posT, wT), cap, base in zip(streams, routed, caps, bases)]


def kernel(x, c, ctx, c_ctx, w_mod, b_mod, w_in, q_norm, k_norm, ssd_conv_w, ssd_conv_b, ssd_a_log, ssd_dt_bias,
           ssd_d, ssd_norm, sc_conv_w, w_br_attn, w_br_ssd, w_br_conv, w_o, ln1_g, ln1_b, w_router, w_exp_gate,
           w_exp_up, w_exp_down, ln2_g, ln2_b):
    b, n, d = x.shape
    nctx = ctx.shape[1]
    depth = w_mod.shape[0]
    alpha = (2 * depth) ** 0.25

    rows = -(-(b + 1) // 8) * 8
    cvec = jnp.concatenate([c, c_ctx[None], jnp.zeros((rows - b - 1, d), F32)], axis=0)
    mods = _mod_call(cvec, w_mod, b_mod).reshape(depth * rows * N_MOD, 1, d)

    cos_l, sin_l = _rope_tables(n)
    cos_c, sin_c = jnp.ones((nctx, LANES), F32), jnp.zeros((nctx, LANES), F32)
    hid = jnp.arange(ATTN_WIDTH) // HEAD_DIM
    bd = (hid[:, None] == hid[None, :]).astype(BF16)
    dexp_all = jnp.repeat(ssd_d, SSD_HEAD_DIM, axis=1)
    pad16 = lambda v: jnp.pad(v.reshape(1, 2 * SSD_HEADS), ((0, 0), (0, LANES - 2 * SSD_HEADS)))
    col16 = lambda v: jnp.broadcast_to(v.reshape(2 * SSD_HEADS, 1), (2 * SSD_HEADS, LANES))

    w_in_t = jnp.swapaxes(w_in, 1, 2)
    w_main = w_in_t.astype(BF16)
    dt_lo, dt_hi = _REF_COLS["dt"]
    w_dt = jnp.pad(w_in_t[:, dt_lo:dt_hi, :], ((0, 0), (0, LANES - (dt_hi - dt_lo)), (0, 0)))
    x_ctx = ctx
    for i in range(depth):
        need_ctx = i < depth - 1
        gq = jnp.tile(q_norm[i], ATTN_HEADS)[None]
        gk = jnp.tile(k_norm[i], KV_HEADS)[None]
        brow, arow = pad16(ssd_dt_bias[i]), pad16(ssd_a_log[i])
        bcol, acol = col16(ssd_dt_bias[i]), col16(ssd_a_log[i])
        dexp, nw = dexp_all[i][None], ssd_norm[i][None]
        wa, ws, wc, wo = (w.astype(BF16) for w in (w_br_attn[i], w_br_ssd[i], w_br_conv[i], w_o[i]))
        wrT = w_router[i].T
        lg1, lb1, lg2, lb2 = ln1_g[i][None], ln1_b[i][None], ln2_g[i][None], ln2_b[i][None]
        mod_l = lambda bi, i=i: (i * rows + bi) * N_MOD
        mod_c = lambda bi, i=i: (i * rows + b) * N_MOD

        def mixer(xin, mod, cos, sin, rope, s0, groups=_MAIN_ORDER):
            t = xin.shape[1]
            main, dt, dtT = _inproj_call(xin, mods, mod, w_main, i, w_dt, tm=_tile(t, 512), groups=groups)
            qh, kh, vh = _qkprep_call(main, cos, sin, gq, gk, bd, tm=_tile(t, 512), rope=rope)
            xa, yc = _conv_call(main, ssd_conv_w[i], ssd_conv_b[i][None], sc_conv_w[i], tm=_tile(t, 512))
            ys, s_out = _ssd_call(xa, dt, dtT, main, s0, brow, arow, bcol, acol, dexp, nw)
            return main, qh, kh, vh, yc, ys, s_out

        def merge(xin, mod, main, ya, ys, yc):
            return _merge_call(ya, ys, yc, main, xin, mods, mod, wa, ws, wc, wo, lg1, lb1, wrT,
                               alpha, tm=_tile(xin.shape[1], 512))

        zero_state = jnp.zeros((b, 2, SSD_GN, SSD_INNER), F32)
        ctx_groups = _MAIN_ORDER if need_ctx else ("k", "v", "xbc")
        main_c, qh_c, kh_c, vh_c, yc_c, ys_c, s_ctx = mixer(x_ctx, mod_c, cos_c, sin_c, False, zero_state, ctx_groups)
        main_l, qh_l, kh_l, vh_l, yc_l, ys_l, _ = mixer(x, mod_l, cos_l, sin_l, True, s_ctx)
        bound = math.sqrt(HEAD_DIM) * jnp.max(jnp.abs(q_norm[i])) * jnp.max(jnp.abs(k_norm[i]))
        ya_l = _attn_call(qh_l, [(kh_c, vh_c), (kh_l, vh_l)], bound, tq=_tile(n, 256))
        streams = [(*merge(x, mod_l, main_l, ya_l, ys_l, yc_l), mod_l)]
        if need_ctx:
            ya_c = _attn_call(qh_c, [(kh_c, vh_c)], bound, tq=_tile(nctx, 256))
            streams.append((*merge(x_ctx, mod_c, main_c, ya_c, ys_c, yc_c), mod_c))
        new = _moe(streams, mods, w_exp_gate, w_exp_up, w_exp_down, i, lg2, lb2, alpha)
        x = new[0]
        if need_ctx:
            x_ctx = new[1]
    return x
```

```python
import functools
import math

import jax
import jax.numpy as jnp
from jax import lax
from jax.experimental import pallas as pl
from jax.experimental.pallas import tpu as pltpu

F32, BF16, I32 = jnp.float32, jnp.bfloat16, jnp.int32

HEAD_DIM = 64
ATTN_HEADS = 8
KV_HEADS = 2
ATTN_REP = ATTN_HEADS // KV_HEADS
ATTN_WIDTH = ATTN_HEADS * HEAD_DIM
KV_WIDTH = KV_HEADS * HEAD_DIM
AXIS_ROT = HEAD_DIM // 2
ROPE_THETA = 10000.0
GRID_W = 64
SSD_HEADS = 8
SSD_HEAD_DIM = 64
SSD_INNER = SSD_HEADS * SSD_HEAD_DIM
SSD_STATE = 64
SSD_GROUPS = 2
SSD_GN = SSD_GROUPS * SSD_STATE
SSD_CONV_DIM = SSD_INNER + 2 * SSD_GN
SSD_CHUNK = 128
SC_WIDTH = 512
N_BRANCH = 3
N_EXPERTS = 16
CAPACITY_FACTOR = 2
N_MOD = 6
EPS = 1e-6

LANES = 128
BF16_ROWS = 16
VMEM_LIMIT_BYTES = 56 * 1024 * 1024

COL_GATE, COL_XBC, COL_KV, COL_Q, COL_Z, COL_SCB, COL_SCC, COL_SCX = 0, 3072, 3840, 4096, 4608, 5120, 5632, 6144
MAIN_COLS = 6656

_NT = (((1,), (1,)), ((), ()))
_TN = (((0,), (0,)), ((), ()))


def _cparams(*sem):
    return pltpu.CompilerParams(dimension_semantics=sem, vmem_limit_bytes=VMEM_LIMIT_BYTES)


def _dot(a, b):
    return jnp.dot(a, b, preferred_element_type=F32)


def _split2(a):
    hi = a.astype(BF16)
    lo = (a - hi.astype(F32)).astype(BF16)
    return hi, lo


def _split3(a):
    p1 = a.astype(BF16)
    r1 = a - p1.astype(F32)
    p2 = r1.astype(BF16)
    p3 = (r1 - p2.astype(F32)).astype(BF16)
    return p1, p2, p3


def _dot_hp(a, w):
    ah, al = _split2(a)
    wh, wl = _split2(w)
    return _dot(ah, wh) + _dot(al, wh) + _dot(ah, wl)


def _dot_hp_nt(a, w):
    ah, al = _split2(a)
    wh, wl = _split2(w)
    d = lambda x, y: lax.dot_general(x, y, _NT, preferred_element_type=F32)
    return d(ah, wh) + d(al, wh) + d(ah, wl)


def _dot_exact_lhs(a, m):
    p1, p2, p3 = _split3(a)
    return _dot(p1, m) + _dot(p2, m) + _dot(p3, m)


def _layer_norm(x):
    mu = jnp.mean(x, axis=-1, keepdims=True)
    xc = x - mu
    var = jnp.mean(xc * xc, axis=-1, keepdims=True)
    return xc * lax.rsqrt(var + EPS)


def _softplus(x):
    return jnp.maximum(x, 0.0) + jnp.log(1.0 + jnp.exp(-jnp.abs(x)))


def _silu(x):
    return x * jax.nn.sigmoid(x)


def _mod_kernel(c_ref, w_ref, b_ref, o_ref):
    o_ref[0] = _dot_hp(_silu(c_ref[...]), w_ref[0]) + b_ref[0]


def _mod_call(cvec, w_mod, b_mod):
    depth, d, n = w_mod.shape
    rows = cvec.shape[0]
    tn = 1536
    return pl.pallas_call(
        _mod_kernel,
        grid=(depth, n // tn),
        in_specs=[pl.BlockSpec((rows, d), lambda l, j: (0, 0)),
                  pl.BlockSpec((1, d, tn), lambda l, j: (l, 0, j)),
                  pl.BlockSpec((1, 1, tn), lambda l, j: (l, 0, j))],
        out_specs=pl.BlockSpec((1, rows, tn), lambda l, j: (l, 0, j)),
        out_shape=jax.ShapeDtypeStruct((depth, rows, n), F32),
        compiler_params=_cparams("parallel", "parallel"),
        name="mod",
    )(cvec, w_mod, b_mod.reshape(depth, 1, n))


_REF_COLS = dict(k=(0, 128), v=(128, 256), xbc=(256, 1024), dt=(1024, 1040), q=(1040, 1552), z=(1552, 2064),
                 scb=(2064, 2576), scc=(2576, 3088), scx=(3088, 3600), gate=(3600, 6672))
_MAIN_ORDER = ("gate", "xbc", "k", "v", "q", "z", "scb", "scc", "scx")


def _inproj_kernel(x_ref, sh_ref, sc_ref, w_ref, wdt_ref, o_ref, dt_ref, dtT_ref, *, groups):
    h = _layer_norm(x_ref[0]) * (1.0 + sc_ref[0]) + sh_ref[0]
    hb = h.astype(BF16)
    step, c0 = 512, 0
    for name in _MAIN_ORDER:
        lo, hi = _REF_COLS[name]
        for r0 in range(lo, hi, step):
            r1 = min(r0 + step, hi)
            if name in groups:
                y = lax.dot_general(hb, w_ref[0, r0:r1, :], _NT, preferred_element_type=F32).astype(o_ref.dtype)
            else:
                y = jnp.zeros((hb.shape[0], r1 - r0), o_ref.dtype)
            o_ref[0, :, c0:c0 + r1 - r0] = y
            c0 += r1 - r0
    dt = _dot_hp_nt(h, wdt_ref[0])
    dt_ref[0] = dt
    dtT_ref[0] = dt.T[:2 * SSD_HEADS, :]


def _mod_spec(d, mrow, j, prefetch=False):
    if prefetch:
        return pl.BlockSpec((1, 1, d), lambda bi, i, _: (mrow(bi) + j, 0, 0))
    return pl.BlockSpec((1, 1, d), lambda bi, i: (mrow(bi) + j, 0, 0))


def _inproj_call(x, mods, mrow, w_t, layer, w_dt_t, tm, groups=_MAIN_ORDER):
    b, t, d = x.shape
    return pl.pallas_call(
        functools.partial(_inproj_kernel, groups=groups),
        grid=(b, t // tm),
        in_specs=[pl.BlockSpec((1, tm, d), lambda bi, i: (bi, i, 0)),
                  _mod_spec(d, mrow, 0), _mod_spec(d, mrow, 1),
                  pl.BlockSpec((1, w_t.shape[1], d), lambda bi, i: (layer, 0, 0)),
                  pl.BlockSpec((1, LANES, d), lambda bi, i: (layer, 0, 0))],
        out_specs=[pl.BlockSpec((1, tm, MAIN_COLS), lambda bi, i: (bi, i, 0)),
                   pl.BlockSpec((1, tm, LANES), lambda bi, i: (bi, i, 0)),
                   pl.BlockSpec((1, 2 * SSD_HEADS, tm), lambda bi, i: (bi, 0, i))],
        out_shape=[jax.ShapeDtypeStruct((b, t, MAIN_COLS), BF16),
                   jax.ShapeDtypeStruct((b, t, LANES), F32),
                   jax.ShapeDtypeStruct((b, 2 * SSD_HEADS, t), F32)],
        compiler_params=_cparams("parallel", "parallel"),
        name="inproj",
    )(x, mods, mods, w_t, w_dt_t)


def _head_rms(x, bd, g):
    hi, lo = _split2(x * x)
    ss = _dot(hi, bd) + _dot(lo, bd)
    return x * lax.rsqrt(ss * (1.0 / HEAD_DIM) + EPS) * g


def _rope(x, cos, sin_signed):
    half = AXIS_ROT // 2
    outs = []
    for j in range(x.shape[1] // LANES):
        xj = x[:, j * LANES:(j + 1) * LANES]
        lane = lax.broadcasted_iota(I32, xj.shape, 1)
        partner = jnp.where((lane & half) == 0, pltpu.roll(xj, LANES - half, 1), pltpu.roll(xj, half, 1))
        outs.append(xj * cos + partner * sin_signed)
    return outs[0] if len(outs) == 1 else jnp.concatenate(outs, axis=1)


def _qkprep_kernel(q_ref, kv_ref, cos_ref, sin_ref, gq_ref, gk_ref, bd_ref, qo_ref, ko_ref, vo_ref, *, rope):
    q = _head_rms(q_ref[0].astype(F32), bd_ref[...], gq_ref[...])
    kv = kv_ref[0]
    k = _head_rms(kv[:, :KV_WIDTH].astype(F32), bd_ref[:KV_WIDTH, :KV_WIDTH], gk_ref[...])
    if rope:
        q = _rope(q, cos_ref[...], sin_ref[...])
        k = _rope(k, cos_ref[...], sin_ref[...])
    q = q * (HEAD_DIM ** -0.5)
    for h in range(ATTN_HEADS):
        qo_ref[0, h] = q[:, h * HEAD_DIM:(h + 1) * HEAD_DIM].astype(BF16)
    v = kv[:, KV_WIDTH:].astype(F32)
    lane = lax.broadcasted_iota(I32, v.shape, 1)
    ones_col = jnp.where(lane == HEAD_DIM, 1.0, 0.0)
    for g in range(KV_HEADS):
        ko_ref[0, g] = k[:, g * HEAD_DIM:(g + 1) * HEAD_DIM].astype(BF16)
        vg = v if g == 0 else pltpu.roll(v, LANES - g * HEAD_DIM, 1)
        vo_ref[0, g] = jnp.where(lane < HEAD_DIM, vg, ones_col).astype(BF16)


def _qkprep_call(main, cos, sin, gq, gk, bd, tm, rope):
    b, t, _ = main.shape
    return pl.pallas_call(
        functools.partial(_qkprep_kernel, rope=rope),
        grid=(b, t // tm),
        in_specs=[pl.BlockSpec((1, tm, ATTN_WIDTH), lambda bi, i: (bi, i, COL_Q // ATTN_WIDTH)),
                  pl.BlockSpec((1, tm, 2 * KV_WIDTH), lambda bi, i: (bi, i, COL_KV // (2 * KV_WIDTH))),
                  pl.BlockSpec((tm, LANES), lambda bi, i: (i, 0)),
                  pl.BlockSpec((tm, LANES), lambda bi, i: (i, 0)),
                  pl.BlockSpec((1, ATTN_WIDTH), lambda bi, i: (0, 0)),
                  pl.BlockSpec((1, KV_WIDTH), lambda bi, i: (0, 0)),
                  pl.BlockSpec((ATTN_WIDTH, ATTN_WIDTH), lambda bi, i: (0, 0))],
        out_specs=[pl.BlockSpec((1, ATTN_HEADS, tm, HEAD_DIM), lambda bi, i: (bi, 0, i, 0)),
                   pl.BlockSpec((1, KV_HEADS, tm, HEAD_DIM), lambda bi, i: (bi, 0, i, 0)),
                   pl.BlockSpec((1, KV_HEADS, tm, LANES), lambda bi, i: (bi, 0, i, 0))],
        out_shape=[jax.ShapeDtypeStruct((b, ATTN_HEADS, t, HEAD_DIM), BF16),
                   jax.ShapeDtypeStruct((b, KV_HEADS, t, HEAD_DIM), BF16),
                   jax.ShapeDtypeStruct((b, KV_HEADS, t, LANES), BF16)],
        compiler_params=_cparams("parallel", "parallel"),
        name="qkprep",
    )(main, main, cos, sin, gq, gk, bd)


ATTN_BOUND_LIMIT = 40.0


def _attn_kernel(shift_ref, q_ref, *refs, bounded):
    o_ref = refs[-1]
    sources = list(zip(refs[:-1:2], refs[1:-1:2]))
    outs = []
    for r in range(ATTN_REP):
        q = q_ref[0, r]
        scores = [lax.dot_general(q, k_ref[0, 0], _NT, preferred_element_type=F32) for k_ref, _ in sources]
        if bounded:
            shift = shift_ref[0]
        else:
            shift = functools.reduce(jnp.maximum, [jnp.max(s, axis=-1, keepdims=True) for s in scores])
        acc = sum(_dot(jnp.exp(s - shift).astype(BF16), v_ref[0, 0]) for s, (_, v_ref) in zip(scores, sources))
        outs.append(acc[:, :HEAD_DIM] / acc[:, HEAD_DIM:HEAD_DIM + 1])
    o_ref[0] = jnp.concatenate(outs, axis=1).astype(o_ref.dtype)


def _attn_call(qh, kv_sources, bound, tq):
    b, _, t, _ = qh.shape
    in_specs = [pl.BlockSpec(memory_space=pltpu.SMEM),
                pl.BlockSpec((1, ATTN_REP, tq, HEAD_DIM), lambda bi, g, i: (bi, g, i, 0))]
    args = [bound.reshape(1), qh]
    for kh, vh in kv_sources:
        tk = kh.shape[2]
        in_specs += [pl.BlockSpec((1, 1, tk, HEAD_DIM), lambda bi, g, i: (bi, g, 0, 0)),
                     pl.BlockSpec((1, 1, tk, LANES), lambda bi, g, i: (bi, g, 0, 0))]
        args += [kh, vh]

    def call(bounded):
        return pl.pallas_call(
            functools.partial(_attn_kernel, bounded=bounded),
            grid=(b, KV_HEADS, t // tq),
            in_specs=in_specs,
            out_specs=pl.BlockSpec((1, tq, ATTN_REP * HEAD_DIM), lambda bi, g, i: (bi, i, g)),
            out_shape=jax.ShapeDtypeStruct((b, t, ATTN_WIDTH), BF16),
            compiler_params=_cparams("parallel", "parallel", "parallel"),
            name="attn_bounded" if bounded else "attn_rowmax",
        )(*args)

    return lax.cond(bound <= ATTN_BOUND_LIMIT, lambda: call(True), lambda: call(False))


def _conv3(u, prev_row, next_row, w):
    tm, s = u.shape[0], 8
    rid = lax.broadcasted_iota(I32, (s, u.shape[1]), 0)
    up = pltpu.roll(u, 1, 0)
    up = jnp.concatenate([jnp.where(rid == 0, prev_row, up[:s]), up[s:]], axis=0)
    dn = pltpu.roll(u, tm - 1, 0)
    dn = jnp.concatenate([dn[:tm - s], jnp.where(rid == s - 1, next_row, dn[tm - s:])], axis=0)
    return up * w[0:1, :] + u * w[1:2, :] + dn * w[2:3, :]


def _conv_kernel(x_ref, xp_ref, xn_ref, c_ref, cp_ref, cn_ref, s_ref, sp_ref, sn_ref, b_ref,
                 w_ref, bias_ref, wsc_ref, xo_ref, yo_ref):
    i, n = pl.program_id(1), pl.num_programs(1)
    keep_p = jnp.where(i > 0, 1.0, 0.0)
    keep_n = jnp.where(i < n - 1, 1.0, 0.0)
    last = BF16_ROWS - 1
    f = lambda r: r.astype(F32)
    u = f(x_ref[0])
    y = _conv3(u, f(xp_ref[0, last:last + 1, :]) * keep_p, f(xn_ref[0, 0:1, :]) * keep_n, w_ref[...]) + bias_ref[...]
    xo_ref[0] = _silu(y).astype(xo_ref.dtype)
    cs = f(c_ref[0]) * f(s_ref[0])
    cs_p = f(cp_ref[0, last:last + 1, :]) * f(sp_ref[0, last:last + 1, :]) * keep_p
    cs_n = f(cn_ref[0, 0:1, :]) * f(sn_ref[0, 0:1, :]) * keep_n
    yo_ref[0] = (f(b_ref[0]) * _conv3(cs, cs_p, cs_n, wsc_ref[...])).astype(yo_ref.dtype)


def _conv_call(main, conv_w, conv_b, sc_w, tm):
    b, t, _ = main.shape
    hb = tm // BF16_ROWS
    nh = t // BF16_ROWS

    def centre(width, col):
        return pl.BlockSpec((1, tm, width), lambda bi, i: (bi, i, col // width))

    def prev(width, col):
        return pl.BlockSpec((1, BF16_ROWS, width), lambda bi, i: (bi, jnp.maximum(i * hb - 1, 0), col // width))

    def nxt(width, col):
        return pl.BlockSpec((1, BF16_ROWS, width), lambda bi, i: (bi, jnp.minimum((i + 1) * hb, nh - 1), col // width))

    const = lambda shape: pl.BlockSpec(shape, lambda bi, i: (0, 0))
    return pl.pallas_call(
        _conv_kernel,
        grid=(b, t // tm),
        in_specs=[centre(SSD_CONV_DIM, COL_XBC), prev(SSD_CONV_DIM, COL_XBC), nxt(SSD_CONV_DIM, COL_XBC),
                  centre(SC_WIDTH, COL_SCC), prev(SC_WIDTH, COL_SCC), nxt(SC_WIDTH, COL_SCC),
                  centre(SC_WIDTH, COL_SCX), prev(SC_WIDTH, COL_SCX), nxt(SC_WIDTH, COL_SCX),
                  centre(SC_WIDTH, COL_SCB),
                  const((3, SSD_CONV_DIM)), const((1, SSD_CONV_DIM)), const((3, SC_WIDTH))],
        out_specs=[pl.BlockSpec((1, tm, SSD_CONV_DIM), lambda bi, i: (bi, i, 0)),
                   pl.BlockSpec((1, tm, SC_WIDTH), lambda bi, i: (bi, i, 0))],
        out_shape=[jax.ShapeDtypeStruct((b, t, SSD_CONV_DIM), BF16),
                   jax.ShapeDtypeStruct((b, t, SC_WIDTH), BF16)],
        compiler_params=_cparams("parallel", "parallel"),
        name="conv",
    )(main, main, main, main, main, main, main, main, main, main, conv_w, conv_b, sc_w)


def _ssd_kernel(xa_ref, dt_ref, dtT_ref, z_ref, s0_ref, brow_ref, arow_ref, bcol_ref, acol_ref, dexp_ref, nw_ref,
                y_ref, so_ref, yf_ref, st_ref):
    L = SSD_CHUNK
    t = xa_ref.shape[1]
    nc = t // L
    ri = lax.broadcasted_iota(I32, (L, L), 0)
    ci = lax.broadcasted_iota(I32, (L, L), 1)
    lane512 = lax.broadcasted_iota(I32, (L, SSD_INNER), 1)
    row512 = lax.broadcasted_iota(I32, (L, SSD_INNER), 0)
    blockmask = jnp.where((row512 >> 6) == (lane512 >> 8), 1.0, 0.0)
    lane128 = lax.broadcasted_iota(I32, (L, LANES), 1)
    a_row = -jnp.exp(arow_ref[...])
    a_col = -jnp.exp(acol_ref[...])

    st_ref[...] = s0_ref[0]

    def chunk(c, d):
        r0 = pl.multiple_of(c * L, L)
        causal = (ri >= ci) if d == 0 else (ri <= ci)
        tri = jnp.where(causal, 1.0, 0.0).astype(BF16)
        triT = jnp.where((ri <= ci) if d == 0 else (ri >= ci), 1.0, 0.0).astype(BF16)
        expand = jnp.where(row512 == (lane512 >> 6) + d * SSD_HEADS, 1.0, 0.0).astype(BF16)
        xa = xa_ref[0, pl.ds(r0, L), :]
        x = xa[:, :SSD_INNER]
        b_pair = xa[:, SSD_INNER:SSD_INNER + SSD_GN]
        c_pair = xa[:, SSD_INNER + SSD_GN:]
        dt_all = _softplus(dt_ref[0, pl.ds(r0, L), :] + brow_ref[...])
        a1, a2, a3 = _split3(dt_all * a_row)
        acs = _dot(tri, a1) + _dot(tri, a2) + _dot(tri, a3)
        dtT = _softplus(dtT_ref[0, :, pl.ds(r0, L)] + bcol_ref[...])
        acsT = _dot_exact_lhs(dtT * a_col, triT)
        edge = acs[L - 1:L, :] if d == 0 else acs[0:1, :]
        w_all = jnp.exp(edge - acs) * dt_all
        e_exp = jnp.exp(_dot_exact_lhs(acs, expand))
        w_exp = _dot(w_all.astype(BF16), expand)
        state = st_ref[d]
        y_off = _dot(c_pair, state.astype(BF16)) * e_exp
        gmats = [lax.dot_general(jnp.where((lane128 >> 6) == g, c_pair, jnp.zeros_like(c_pair)), b_pair, _NT,
                                 preferred_element_type=F32) for g in range(SSD_GROUPS)]
        pairs = []
        for pr in range(SSD_HEADS // 2):
            x_pair = x[:, pr * LANES:(pr + 1) * LANES]
            acc = jnp.zeros((L, LANES), F32)
            for hh in range(2):
                h = 2 * pr + hh
                gmat = gmats[h // (SSD_HEADS // SSD_GROUPS)]
                k = d * SSD_HEADS + h
                seg = acs[:, k:k + 1] - acsT[k:k + 1, :]
                m = jnp.exp(jnp.where(causal, seg, -jnp.inf)) * gmat * dtT[k:k + 1, :]
                xh = jnp.where((lane128 >> 6) == hh, x_pair, jnp.zeros_like(x_pair))
                acc = acc + _dot(m.astype(BF16), xh)
            pairs.append(acc)
        y = y_off + jnp.concatenate(pairs, axis=1)
        e_edge = e_exp[L - 1:L, :] if d == 0 else e_exp[0:1, :]
        ds = lax.dot_general(b_pair, (x.astype(F32) * w_exp).astype(BF16), _TN, preferred_element_type=F32)
        st_ref[d] = (state * e_edge + ds) * blockmask
        return y, x, r0

    def fwd(c, carry):
        y, _, r0 = chunk(c, 0)
        yf_ref[pl.ds(r0, L), :] = y
        return carry

    unroll = 4 if nc % 4 == 0 else (2 if nc % 2 == 0 else 1)
    lax.fori_loop(0, nc, fwd, 0, unroll=unroll)

    def bwd(i, carry):
        c = nc - 1 - i
        yb, x, r0 = chunk(c, 1)
        y = yf_ref[pl.ds(r0, L), :] + yb + dexp_ref[...] * x.astype(F32)
        gated = y * _silu(z_ref[0, pl.ds(r0, L), :].astype(F32))
        ms = jnp.mean(gated * gated, axis=-1, keepdims=True)
        y_ref[0, pl.ds(r0, L), :] = (gated * lax.rsqrt(ms + EPS) * nw_ref[...]).astype(y_ref.dtype)
        return carry

    lax.fori_loop(0, nc, bwd, 0, unroll=unroll)
    so_ref[0] = st_ref[...]


def _ssd_call(xa, dt, dtT, main, s0, brow, arow, bcol, acol, dexp, nw):
    b, t, _ = xa.shape
    c2 = lambda shape: pl.BlockSpec(shape, lambda bi: (0, 0))
    return pl.pallas_call(
        _ssd_kernel,
        grid=(b,),
        in_specs=[pl.BlockSpec((1, t, SSD_CONV_DIM), lambda bi: (bi, 0, 0)),
                  pl.BlockSpec((1, t, LANES), lambda bi: (bi, 0, 0)),
                  pl.BlockSpec((1, 2 * SSD_HEADS, t), lambda bi: (bi, 0, 0)),
                  pl.BlockSpec((1, t, SSD_INNER), lambda bi: (bi, 0, COL_Z // SSD_INNER)),
                  pl.BlockSpec((1, 2, SSD_GN, SSD_INNER), lambda bi: (bi, 0, 0, 0)),
                  c2((1, LANES)), c2((1, LANES)), c2((2 * SSD_HEADS, LANES)), c2((2 * SSD_HEADS, LANES)),
                  c2((1, SSD_INNER)), c2((1, SSD_INNER))],
        out_specs=[pl.BlockSpec((1, t, SSD_INNER), lambda bi: (bi, 0, 0)),
                   pl.BlockSpec((1, 2, SSD_GN, SSD_INNER), lambda bi: (bi, 0, 0, 0))],
        out_shape=[jax.ShapeDtypeStruct((b, t, SSD_INNER), BF16),
                   jax.ShapeDtypeStruct((b, 2, SSD_GN, SSD_INNER), F32)],
        scratch_shapes=[pltpu.VMEM((t, SSD_INNER), F32), pltpu.VMEM((2, SSD_GN, SSD_INNER), F32)],
        compiler_params=_cparams("parallel"),
        name="ssd",
    )(xa, dt, dtT, main, s0, brow, arow, bcol, acol, dexp, nw)


def _merge_kernel(ya_ref, ys_ref, yc_ref, g_ref, x_ref, m2_ref, m3_ref, m4_ref, wa_ref, ws_ref, wc_ref, wo_ref,
                  lg_ref, lb_ref, wr_ref, x1_ref, h2_ref, aff_ref, *, alpha):
    d = x_ref.shape[2]
    gate = lambda j: jax.nn.sigmoid(g_ref[0, :, j * d:(j + 1) * d].astype(F32))
    m = (gate(0) * _dot(ya_ref[0], wa_ref[...]) + gate(1) * _dot(ys_ref[0], ws_ref[...])
         + gate(2) * _dot(yc_ref[0], wc_ref[...]))
    out = _dot(m.astype(BF16), wo_ref[...])
    x1 = _layer_norm(alpha * x_ref[0] + m2_ref[0] * out) * lg_ref[...] + lb_ref[...]
    x1_ref[0] = x1
    h2 = _layer_norm(x1) * (1.0 + m4_ref[0]) + m3_ref[0]
    h2_ref[0] = h2.astype(h2_ref.dtype)
    logits = _dot_hp_nt(wr_ref[...], h2)
    e = jnp.exp(logits - jnp.max(logits, axis=0, keepdims=True))
    aff_ref[0] = e / jnp.sum(e, axis=0, keepdims=True)


def _merge_call(ya, ys, yc, main, x, mods, mrow, wa, ws, wc, wo, lg, lb, wrT, alpha, tm):
    b, t, d = x.shape
    tok = lambda w: pl.BlockSpec((1, tm, w), lambda bi, i: (bi, i, 0))
    const = lambda shape: pl.BlockSpec(shape, lambda bi, i: (0, 0))
    return pl.pallas_call(
        functools.partial(_merge_kernel, alpha=alpha),
        grid=(b, t // tm),
        in_specs=[tok(ATTN_WIDTH), tok(SSD_INNER), tok(SC_WIDTH), tok(N_BRANCH * d), tok(d),
                  _mod_spec(d, mrow, 2), _mod_spec(d, mrow, 3), _mod_spec(d, mrow, 4),
                  const((ATTN_WIDTH, d)), const((SSD_INNER, d)), const((SC_WIDTH, d)), const((d, d)),
                  const((1, d)), const((1, d)), const((N_EXPERTS, d))],
        out_specs=[tok(d), tok(d), pl.BlockSpec((1, N_EXPERTS, tm), lambda bi, i: (bi, 0, i))],
        out_shape=[jax.ShapeDtypeStruct((b, t, d), F32), jax.ShapeDtypeStruct((b, t, d), BF16),
                   jax.ShapeDtypeStruct((b, N_EXPERTS, t), F32)],
        compiler_params=_cparams("parallel", "parallel"),
        name="merge",
    )(ya, ys, yc, main, x, mods, mods, mods, wa, ws, wc, wo, lg, lb, wrT)


def _route_kernel(aff_ref, pos_ref, w_ref, offs_ref, posT_ref, wT_ref, *, cap):
    aff = aff_ref[0]
    ne, t = aff.shape
    capf = float(cap)

    def enough(cand):
        return jnp.sum(jnp.where(aff >= pltpu.bitcast(cand, F32), 1.0, 0.0), axis=1, keepdims=True) >= capf

    def step(i, lo):
        b0 = jnp.left_shift(jnp.int32(1), 28 - 2 * i)
        b1 = b0 + b0
        c01, c10, c11 = lo | b0, lo | b1, lo | b1 | b0
        return jnp.where(enough(c11), c11, jnp.where(enough(c10), c10, jnp.where(enough(c01), c01, lo)))

    lo = jnp.zeros((ne, 1), I32)
    top = lo | jnp.int32(1 << 30)
    lo = jnp.where(enough(top), top, lo)
    thr = pltpu.bitcast(lax.fori_loop(0, 15, step, lo), F32)
    need = capf - jnp.sum(jnp.where(aff > thr, 1.0, 0.0), axis=1, keepdims=True)
    ri = lax.broadcasted_iota(I32, (LANES, LANES), 0)
    ci = lax.broadcasted_iota(I32, (LANES, LANES), 1)
    upper = jnp.where(ri < ci, 1.0, 0.0).astype(BF16)
    lane = lax.broadcasted_iota(I32, (ne, LANES), 1)
    carry_gt = jnp.zeros((ne, 1), F32)
    carry_eq = jnp.zeros((ne, 1), F32)
    offs = jnp.zeros((ne, LANES), F32)
    nblk = t // LANES
    for j in range(nblk):
        sl = slice(j * LANES, (j + 1) * LANES)
        aj = aff[:, sl]
        gj = jnp.where(aj > thr, 1.0, 0.0)
        ej = jnp.where(aj == thr, 1.0, 0.0)
        pre = _dot(jnp.concatenate([gj, ej], axis=0).astype(BF16), upper)
        pre_gt = pre[:ne] + carry_gt
        pre_eq = pre[ne:] + carry_eq
        sel = gj + ej * jnp.where(pre_eq < need, 1.0, 0.0)
        slot = pre_gt + jnp.minimum(pre_eq, need)
        pos_j = jnp.where(sel > 0.5, slot, -1.0)
        w_j = jnp.where(sel > 0.5, aj, 0.0)
        pos_ref[0, :, sl] = pos_j
        w_ref[0, :, sl] = w_j
        fill = jnp.zeros((LANES - ne, LANES), F32)
        posT_ref[0, sl, :] = jnp.concatenate([pos_j, fill], axis=0).T[:, :ne]
        wT_ref[0, sl, :] = jnp.concatenate([w_j, fill], axis=0).T[:, :ne]
        offs = jnp.where(lane == j, carry_gt + jnp.minimum(carry_eq, need), offs)
        carry_gt = carry_gt + jnp.sum(gj, axis=1, keepdims=True)
        carry_eq = carry_eq + jnp.sum(ej, axis=1, keepdims=True)
    offs = jnp.where(lane >= nblk, capf, offs)
    offs_ref[0] = offs.astype(I32)


def _route_call(affT, cap):
    b, ne, t = affT.shape
    spec = pl.BlockSpec((1, ne, t), lambda bi: (bi, 0, 0))
    spec_t = pl.BlockSpec((1, t, ne), lambda bi: (bi, 0, 0))
    return pl.pallas_call(
        functools.partial(_route_kernel, cap=cap),
        grid=(b,),
        in_specs=[spec],
        out_specs=[spec, spec, pl.BlockSpec((1, ne, LANES), lambda bi: (bi, 0, 0)), spec_t, spec_t],
        out_shape=[jax.ShapeDtypeStruct((b, ne, t), F32), jax.ShapeDtypeStruct((b, ne, t), F32),
                   jax.ShapeDtypeStruct((b, ne, LANES), I32),
                   jax.ShapeDtypeStruct((b, t, ne), F32), jax.ShapeDtypeStruct((b, t, ne), F32)],
        compiler_params=_cparams("parallel"),
        name="route",
    )(affT)


EXPERT_GROUP = 4


def _slot_windows(offs_ref, bi, i, ne, per, cap, win):
    out = []
    for e in range(ne):
        lo = offs_ref[(bi * ne + e) * LANES + i * per]
        hi = offs_ref[(bi * ne + e) * LANES + (i + 1) * per]
        out.append((lo, hi, lo - lax.rem(lo, BF16_ROWS)))
    return out


def _window_start(wb, p, cap, win):
    return pl.multiple_of(jnp.minimum(wb + p * win, cap - win), BF16_ROWS)


def _gather_kernel(offs_ref, h_ref, pos_ref, *rest, win):
    xs_ref = rest[-1]
    bi, i = pl.program_id(0), pl.program_id(1)
    tm = h_ref.shape[1]
    ne, cap = xs_ref.shape[1], xs_ref.shape[2]
    windows = _slot_windows(offs_ref, bi, i, ne, tm // LANES, cap, win)
    h = h_ref[0]
    pos = pos_ref[0]
    row = lax.broadcasted_iota(I32, (win, tm), 0)

    @pl.when(i == 0)
    def _():
        xs_ref[...] = jnp.zeros_like(xs_ref)

    def onehot(e, start, first=None):
        slot = row + start
        if first is not None:
            slot = jnp.where(slot >= first, slot, -2)
        return jnp.where(pos[e:e + 1, :] == slot.astype(F32), 1.0, 0.0).astype(BF16)

    def add_window(e, start, rows):
        xs_ref[0, e, pl.ds(start, win), :] += rows.astype(xs_ref.dtype)

    for g0 in range(0, ne, EXPERT_GROUP):
        experts = range(g0, min(g0 + EXPERT_GROUP, ne))
        starts = [_window_start(windows[e][2], 0, cap, win) for e in experts]
        y = _dot(jnp.concatenate([onehot(e, s) for e, s in zip(experts, starts)], axis=0), h)
        for k, (e, s) in enumerate(zip(experts, starts)):
            add_window(e, s, y[k * win:(k + 1) * win])
    for p in range(1, pl.cdiv(cap, win)):
        for e in range(ne):
            lo, hi, wb = windows[e]

            @pl.when(hi > wb + p * win)
            def _():
                s = _window_start(wb, p, cap, win)
                add_window(e, s, _dot(onehot(e, s, first=wb + p * win), h))


def _gather_call(offs, h2, pos, cap, tm, slots_total, slot_base, buf=None):
    b, t, d = h2.shape
    ne = pos.shape[1]
    mean = tm * cap // t
    win = min(LANES, cap, pl.cdiv(mean + mean // 4 + BF16_ROWS, BF16_ROWS) * BF16_ROWS)
    in_specs = [pl.BlockSpec((1, tm, d), lambda bi, i, offs: (bi, i, 0)),
                pl.BlockSpec((1, ne, tm), lambda bi, i, offs: (bi, 0, i))]
    args = [offs, h2, pos]
    if buf is not None:
        in_specs.append(pl.BlockSpec(memory_space=pl.ANY))
        args.append(buf)
    return pl.pallas_call(
        functools.partial(_gather_kernel, win=win),
        grid_spec=pltpu.PrefetchScalarGridSpec(
            num_scalar_prefetch=1,
            grid=(b, t // tm),
            in_specs=in_specs,
            out_specs=pl.BlockSpec((1, ne, cap, d), lambda bi, i, offs: (bi, 0, slot_base // cap, 0))),
        out_shape=jax.ShapeDtypeStruct((b, ne, slots_total, d), BF16),
        input_output_aliases={} if buf is None else {3: 0},
        compiler_params=_cparams("parallel", "arbitrary"),
        name="gather",
    )(*args)


def _ffn_kernel(xs_ref, wg_ref, wu_ref, wd_ref, o_ref, acc_ref, *, group):
    f, nf = pl.program_id(1), pl.num_programs(1)
    b, _, slots, d = xs_ref.shape
    kc = 256

    def dot_w(a, w_ref):
        k = w_ref.shape[2]
        return sum(_dot(a[:, k0:k0 + kc], w_ref[0, 0, k0:k0 + kc, :].astype(BF16)) for k0 in range(0, k, kc))

    @pl.when(f == 0)
    def _():
        acc_ref[...] = jnp.zeros_like(acc_ref)

    for b0 in range(0, b, group):
        x = xs_ref[b0:b0 + group, 0].reshape(group * slots, d)
        hid = (_silu(dot_w(x, wg_ref)) * dot_w(x, wu_ref)).astype(BF16)
        acc_ref[b0:b0 + group] += dot_w(hid, wd_ref).reshape(group, slots, d)

    @pl.when(f == nf - 1)
    def _():
        o_ref[:, 0] = acc_ref[...].astype(o_ref.dtype)


def _ffn_call(xs, wg, wu, wd, layer, tf):
    b, ne, slots, d = xs.shape
    ff = wg.shape[3]
    tok = pl.BlockSpec((b, 1, slots, d), lambda e, f: (0, e, 0, 0))
    return pl.pallas_call(
        functools.partial(_ffn_kernel, group=1),
        grid=(ne, ff // tf),
        in_specs=[tok,
                  pl.BlockSpec((1, 1, d, tf), lambda e, f: (layer, e, 0, f)),
                  pl.BlockSpec((1, 1, d, tf), lambda e, f: (layer, e, 0, f)),
                  pl.BlockSpec((1, 1, tf, d), lambda e, f: (layer, e, f, 0))],
        out_specs=tok,
        out_shape=jax.ShapeDtypeStruct(xs.shape, BF16),
        scratch_shapes=[pltpu.VMEM((b, slots, d), F32)],
        compiler_params=_cparams("parallel", "arbitrary"),
        name="ffn",
    )(xs, wg, wu, wd)


def _combine_kernel(offs_ref, o_ref, pos_ref, w_ref, x_ref, m5_ref, lg_ref, lb_ref, y_ref, acc_ref, *, alpha, win):
    bi, i = pl.program_id(0), pl.program_id(1)
    tm = x_ref.shape[1]
    ne, cap = o_ref.shape[1], o_ref.shape[2]
    windows = _slot_windows(offs_ref, bi, i, ne, tm // LANES, cap, win)
    pos = pos_ref[0]
    w = w_ref[0]
    col = lax.broadcasted_iota(I32, (tm, win), 1)

    def onehot(e, start, first=None):
        slot = col + start
        if first is not None:
            slot = jnp.where(slot >= first, slot, -2)
        return jnp.where(pos[:, e:e + 1] == slot.astype(F32), w[:, e:e + 1], 0.0).astype(BF16)

    y = jnp.zeros((tm, o_ref.shape[3]), F32)
    for g0 in range(0, ne, EXPERT_GROUP):
        experts = range(g0, min(g0 + EXPERT_GROUP, ne))
        starts = [_window_start(windows[e][2], 0, cap, win) for e in experts]
        lhs = jnp.concatenate([onehot(e, s) for e, s in zip(experts, starts)], axis=1)
        rhs = jnp.concatenate([o_ref[0, e, pl.ds(s, win), :] for e, s in zip(experts, starts)], axis=0)
        y = y + _dot(lhs, rhs)
    acc_ref[...] = y
    for p in range(1, cap // win):
        for e in range(ne):
            lo, hi, wb = windows[e]

            @pl.when(hi > wb + p * win)
            def _():
                s = _window_start(wb, p, cap, win)
                acc_ref[...] += _dot(onehot(e, s, first=wb + p * win), o_ref[0, e, pl.ds(s, win), :])
    y_ref[0] = _layer_norm(alpha * x_ref[0] + m5_ref[0] * acc_ref[...]) * lg_ref[...] + lb_ref[...]


def _combine_call(offs, out, posT, wT, x1, mods, mrow, lg, lb, alpha, tm, cap, slot_base):
    b, t, d = x1.shape
    ne = out.shape[1]
    return pl.pallas_call(
        functools.partial(_combine_kernel, alpha=alpha, win=min(LANES, cap)),
        grid_spec=pltpu.PrefetchScalarGridSpec(
            num_scalar_prefetch=1,
            grid=(b, t // tm),
            in_specs=[pl.BlockSpec((1, ne, cap, d), lambda bi, i, offs: (bi, 0, slot_base // cap, 0)),
                      pl.BlockSpec((1, tm, ne), lambda bi, i, offs: (bi, i, 0)),
                      pl.BlockSpec((1, tm, ne), lambda bi, i, offs: (bi, i, 0)),
                      pl.BlockSpec((1, tm, d), lambda bi, i, offs: (bi, i, 0)),
                      _mod_spec(d, mrow, 5, prefetch=True),
                      pl.BlockSpec((1, d), lambda bi, i, offs: (0, 0)),
                      pl.BlockSpec((1, d), lambda bi, i, offs: (0, 0))],
            out_specs=pl.BlockSpec((1, tm, d), lambda bi, i, offs: (bi, i, 0)),
            scratch_shapes=[pltpu.VMEM((tm, d), F32)]),
        out_shape=jax.ShapeDtypeStruct((b, t, d), F32),
        compiler_params=_cparams("parallel", "arbitrary"),
        name="combine",
    )(offs, out, posT, wT, x1, mods, lg, lb)


def _rope_tables(n):
    rows = n // GRID_W
    row = jnp.repeat(jnp.arange(rows), GRID_W).astype(F32)
    col = jnp.tile(jnp.arange(GRID_W), rows).astype(F32)
    inv = ROPE_THETA ** (-jnp.arange(0, AXIS_ROT, 2, dtype=F32) / AXIS_ROT)
    ang = jnp.stack([row[:, None] * inv, col[:, None] * inv], axis=1)
    cos = jnp.repeat(jnp.cos(ang)[:, :, None, :], 2, axis=2).reshape(n, HEAD_DIM)
    sin = jnp.stack([-jnp.sin(ang), jnp.sin(ang)], axis=2).reshape(n, HEAD_DIM)
    return jnp.tile(cos, (1, LANES // HEAD_DIM)), jnp.tile(sin, (1, LANES // HEAD_DIM))


def _tile(t, pref):
    return pref if t % pref == 0 else t


def _moe(streams, mods, wg, wu, wd, layer, lg, lb, alpha):
    caps = [CAPACITY_FACTOR * h2.shape[1] // N_EXPERTS for _, h2, _, _ in streams]
    bases = [sum(caps[:k]) for k in range(len(caps))]
    assert all(base % cap == 0 for base, cap in zip(bases, caps)) and sum(caps) % BF16_ROWS == 0
    xs, routed = None, []
    for (_, h2, affT, _), cap, base in zip(streams, caps, bases):
        t = h2.shape[1]
        pos, _, offs, posT, wT = _route_call(affT, cap)
        offs = offs.reshape(-1)
        xs = _gather_call(offs, h2, pos, cap, tm=_tile(t, 512), slots_total=sum(caps), slot_base=base, buf=xs)
        routed.append((offs, posT, wT))
    out = _ffn_call(xs, wg, wu, wd, layer, tf=_tile(wg.shape[3], 512))
    return [_combine_call(offs, out, posT, wT, x1, mods, mrow, lg, lb, alpha, tm=_tile(x1.shape[1], 512),
                          cap=cap, slot_base=base)
            for (x1, _, _, mrow), (offs, posT, wT), cap, base in zip(streams, routed, caps, bases)]


def kernel(x, c, ctx, c_ctx, w_mod, b_mod, w_in, q_norm, k_norm, ssd_conv_w, ssd_conv_b, ssd_a_log, ssd_dt_bias,
           ssd_d, ssd_norm, sc_conv_w, w_br_attn, w_br_ssd, w_br_conv, w_o, ln1_g, ln1_b, w_router, w_exp_gate,
           w_exp_up, w_exp_down, ln2_g, ln2_b):
    b, n, d = x.shape
    nctx = ctx.shape[1]
    depth = w_mod.shape[0]
    alpha = (2 * depth) ** 0.25

    rows = -(-(b + 1) // 8) * 8
    cvec = jnp.concatenate([c, c_ctx[None], jnp.zeros((rows - b - 1, d), F32)], axis=0)
    mods = _mod_call(cvec, w_mod, b_mod).reshape(depth * rows * N_MOD, 1, d)

    cos_l, sin_l = _rope_tables(n)
    cos_c, sin_c = jnp.ones((nctx, LANES), F32), jnp.zeros((nctx, LANES), F32)
    hid = jnp.arange(ATTN_WIDTH) // HEAD_DIM
    bd = (hid[:, None] == hid[None, :]).astype(BF16)
    dexp_all = jnp.repeat(ssd_d, SSD_HEAD_DIM, axis=1)
    pad16 = lambda v: jnp.pad(v.reshape(1, 2 * SSD_HEADS), ((0, 0), (0, LANES - 2 * SSD_HEADS)))
    col16 = lambda v: jnp.broadcast_to(v.reshape(2 * SSD_HEADS, 1), (2 * SSD_HEADS, LANES))

    w_in_t = jnp.swapaxes(w_in, 1, 2)
    w_main = w_in_t.astype(BF16)
    dt_lo, dt_hi = _REF_COLS["dt"]
    w_dt = jnp.pad(w_in_t[:, dt_lo:dt_hi, :], ((0, 0), (0, LANES - (dt_hi - dt_lo)), (0, 0)))
    x_ctx = ctx
    for i in range(depth):
        need_ctx = i < depth - 1
        gq = jnp.tile(q_norm[i], ATTN_HEADS)[None]
        gk = jnp.tile(k_norm[i], KV_HEADS)[None]
        brow, arow = pad16(ssd_dt_bias[i]), pad16(ssd_a_log[i])
        bcol, acol = col16(ssd_dt_bias[i]), col16(ssd_a_log[i])
        dexp, nw = dexp_all[i][None], ssd_norm[i][None]
        wa, ws, wc, wo = (w.astype(BF16) for w in (w_br_attn[i], w_br_ssd[i], w_br_conv[i], w_o[i]))
        wrT = w_router[i].T
        lg1, lb1, lg2, lb2 = ln1_g[i][None], ln1_b[i][None], ln2_g[i][None], ln2_b[i][None]
        mod_l = lambda bi, i=i: (i * rows + bi) * N_MOD
        mod_c = lambda bi, i=i: (i * rows + b) * N_MOD

        def mixer(xin, mod, cos, sin, rope, s0, groups=_MAIN_ORDER):
            t = xin.shape[1]
            main, dt, dtT = _inproj_call(xin, mods, mod, w_main, i, w_dt, tm=_tile(t, 512), groups=groups)
            qh, kh, vh = _qkprep_call(main, cos, sin, gq, gk, bd, tm=_tile(t, 512), rope=rope)
            xa, yc = _conv_call(main, ssd_conv_w[i], ssd_conv_b[i][None], sc_conv_w[i], tm=_tile(t, 512))
            ys, s_out = _ssd_call(xa, dt, dtT, main, s0, brow, arow, bcol, acol, dexp, nw)
            return main, qh, kh, vh, yc, ys, s_out

        def merge(xin, mod, main, ya, ys, yc):
            return _merge_call(ya, ys, yc, main, xin, mods, mod, wa, ws, wc, wo, lg1, lb1, wrT,
                               alpha, tm=_tile(xin.shape[1], 512))

        zero_state = jnp.zeros((b, 2, SSD_GN, SSD_INNER), F32)
        ctx_groups = _MAIN_ORDER if need_ctx else ("k", "v", "xbc")
        main_c, qh_c, kh_c, vh_c, yc_c, ys_c, s_ctx = mixer(x_ctx, mod_c, cos_c, sin_c, False, zero_state, ctx_groups)
        main_l, qh_l, kh_l, vh_l, yc_l, ys_l, _ = mixer(x, mod_l, cos_l, sin_l, True, s_ctx)
        bound = math.sqrt(HEAD_DIM) * jnp.max(jnp.abs(q_norm[i])) * jnp.max(jnp.abs(k_norm[i]))
        ya_l = _attn_call(qh_l, [(kh_c, vh_c), (kh_l, vh_l)], bound, tq=_tile(n, 256))
        streams = [(*merge(x, mod_l, main_l, ya_l, ys_l, yc_l), mod_l)]
        if need_ctx:
            ya_c = _attn_call(qh_c, [(kh_c, vh_c)], bound, tq=_tile(nctx, 256))
            streams.append((*merge(x_ctx, mod_c, main_c, ya_c, ys_c, yc_c), mod_c))
        new = _moe(streams, mods, w_exp_gate, w_exp_up, w_exp_down, i, lg2, lb2, alpha)
        x = new[0]
        if need_ctx:
            x_ctx = new[1]
    return x
```

```python
import functools
import math

import jax
import jax.numpy as jnp
from jax import lax
from jax.experimental import pallas as pl
from jax.experimental.pallas import tpu as pltpu

F32, BF16, I32 = jnp.float32, jnp.bfloat16, jnp.int32

HEAD_DIM = 64
ATTN_HEADS = 8
KV_HEADS = 2
ATTN_REP = ATTN_HEADS // KV_HEADS
ATTN_WIDTH = ATTN_HEADS * HEAD_DIM
KV_WIDTH = KV_HEADS * HEAD_DIM
AXIS_ROT = HEAD_DIM // 2
ROPE_THETA = 10000.0
GRID_W = 64
SSD_HEADS = 8
SSD_HEAD_DIM = 64
SSD_INNER = SSD_HEADS * SSD_HEAD_DIM
SSD_STATE = 64
SSD_GROUPS = 2
SSD_GN = SSD_GROUPS * SSD_STATE
SSD_CONV_DIM = SSD_INNER + 2 * SSD_GN
SSD_CHUNK = 128
SC_WIDTH = 512
N_BRANCH = 3
N_EXPERTS = 16
CAPACITY_FACTOR = 2
N_MOD = 6
EPS = 1e-6

LANES = 128
BF16_ROWS = 16
VMEM_LIMIT_BYTES = 56 * 1024 * 1024

COL_GATE, COL_XBC, COL_KV, COL_Q, COL_Z, COL_SCB, COL_SCC, COL_SCX = 0, 3072, 3840, 4096, 4608, 5120, 5632, 6144
MAIN_COLS = 6656

_NT = (((1,), (1,)), ((), ()))
_TN = (((0,), (0,)), ((), ()))


def _cparams(*sem):
    return pltpu.CompilerParams(dimension_semantics=sem, vmem_limit_bytes=VMEM_LIMIT_BYTES)


def _dot(a, b):
    return jnp.dot(a, b, preferred_element_type=F32)


def _split2(a):
    hi = a.astype(BF16)
    lo = (a - hi.astype(F32)).astype(BF16)
    return hi, lo


def _split3(a):
    p1 = a.astype(BF16)
    r1 = a - p1.astype(F32)
    p2 = r1.astype(BF16)
    p3 = (r1 - p2.astype(F32)).astype(BF16)
    return p1, p2, p3


def _dot_hp(a, w):
    ah, al = _split2(a)
    wh, wl = _split2(w)
    return _dot(ah, wh) + _dot(al, wh) + _dot(ah, wl)


def _dot_hp_nt(a, w):
    ah, al = _split2(a)
    wh, wl = _split2(w)
    d = lambda x, y: lax.dot_general(x, y, _NT, preferred_element_type=F32)
    return d(ah, wh) + d(al, wh) + d(ah, wl)


def _dot_exact_lhs(a, m):
    p1, p2, p3 = _split3(a)
    return _dot(p1, m) + _dot(p2, m) + _dot(p3, m)


def _layer_norm(x):
    mu = jnp.mean(x, axis=-1, keepdims=True)
    xc = x - mu
    var = jnp.mean(xc * xc, axis=-1, keepdims=True)
    return xc * lax.rsqrt(var + EPS)


def _softplus(x):
    return jnp.maximum(x, 0.0) + jnp.log(1.0 + jnp.exp(-jnp.abs(x)))


def _silu(x):
    return x * jax.nn.sigmoid(x)


def _mod_kernel(c_ref, w_ref, b_ref, o_ref):
    o_ref[0] = _dot_hp(_silu(c_ref[...]), w_ref[0]) + b_ref[0]


def _mod_call(cvec, w_mod, b_mod):
    depth, d, n = w_mod.shape
    rows = cvec.shape[0]
    tn = 1536
    return pl.pallas_call(
        _mod_kernel,
        grid=(depth, n // tn),
        in_specs=[pl.BlockSpec((rows, d), lambda l, j: (0, 0)),
                  pl.BlockSpec((1, d, tn), lambda l, j: (l, 0, j)),
                  pl.BlockSpec((1, 1, tn), lambda l, j: (l, 0, j))],
        out_specs=pl.BlockSpec((1, rows, tn), lambda l, j: (l, 0, j)),
        out_shape=jax.ShapeDtypeStruct((depth, rows, n), F32),
        compiler_params=_cparams("parallel", "parallel"),
        name="mod",
    )(cvec, w_mod, b_mod.reshape(depth, 1, n))


_REF_COLS = dict(k=(0, 128), v=(128, 256), xbc=(256, 1024), dt=(1024, 1040), q=(1040, 1552), z=(1552, 2064),
                 scb=(2064, 2576), scc=(2576, 3088), scx=(3088, 3600), gate=(3600, 6672))
_MAIN_ORDER = ("gate", "xbc", "k", "v", "q", "z", "scb", "scc", "scx")


def _inproj_kernel(x_ref, sh_ref, sc_ref, w_ref, wdt_ref, o_ref, dt_ref, dtT_ref, *, groups):
    h = _layer_norm(x_ref[0]) * (1.0 + sc_ref[0]) + sh_ref[0]
    hb = h.astype(BF16)
    step, c0 = 512, 0
    for name in _MAIN_ORDER:
        lo, hi = _REF_COLS[name]
        for r0 in range(lo, hi, step):
            r1 = min(r0 + step, hi)
            if name in groups:
                y = lax.dot_general(hb, w_ref[0, r0:r1, :], _NT, preferred_element_type=F32).astype(o_ref.dtype)
            else:
                y = jnp.zeros((hb.shape[0], r1 - r0), o_ref.dtype)
            o_ref[0, :, c0:c0 + r1 - r0] = y
            c0 += r1 - r0
    dt = _dot_hp_nt(h, wdt_ref[0])
    dt_ref[0] = dt
    dtT_ref[0] = dt.T[:2 * SSD_HEADS, :]


def _mod_spec(d, mrow, j, prefetch=False):
    if prefetch:
        return pl.BlockSpec((1, 1, d), lambda bi, i, _: (mrow(bi) + j, 0, 0))
    return pl.BlockSpec((1, 1, d), lambda bi, i: (mrow(bi) + j, 0, 0))


def _inproj_call(x, mods, mrow, w_t, layer, w_dt_t, tm, groups=_MAIN_ORDER):
    b, t, d = x.shape
    return pl.pallas_call(
        functools.partial(_inproj_kernel, groups=groups),
        grid=(b, t // tm),
        in_specs=[pl.BlockSpec((1, tm, d), lambda bi, i: (bi, i, 0)),
                  _mod_spec(d, mrow, 0), _mod_spec(d, mrow, 1),
                  pl.BlockSpec((1, w_t.shape[1], d), lambda bi, i: (layer, 0, 0)),
                  pl.BlockSpec((1, LANES, d), lambda bi, i: (layer, 0, 0))],
        out_specs=[pl.BlockSpec((1, tm, MAIN_COLS), lambda bi, i: (bi, i, 0)),
                   pl.BlockSpec((1, tm, LANES), lambda bi, i: (bi, i, 0)),
                   pl.BlockSpec((1, 2 * SSD_HEADS, tm), lambda bi, i: (bi, 0, i))],
        out_shape=[jax.ShapeDtypeStruct((b, t, MAIN_COLS), BF16),
                   jax.ShapeDtypeStruct((b, t, LANES), F32),
                   jax.ShapeDtypeStruct((b, 2 * SSD_HEADS, t), F32)],
        compiler_params=_cparams("parallel", "parallel"),
        name="inproj",
    )(x, mods, mods, w_t, w_dt_t)


def _head_rms(x, bd, g):
    hi, lo = _split2(x * x)
    ss = _dot(hi, bd) + _dot(lo, bd)
    return x * lax.rsqrt(ss * (1.0 / HEAD_DIM) + EPS) * g


def _rope(x, cos, sin_signed):
    half = AXIS_ROT // 2
    outs = []
    for j in range(x.shape[1] // LANES):
        xj = x[:, j * LANES:(j + 1) * LANES]
        lane = lax.broadcasted_iota(I32, xj.shape, 1)
        partner = jnp.where((lane & half) == 0, pltpu.roll(xj, LANES - half, 1), pltpu.roll(xj, half, 1))
        outs.append(xj * cos + partner * sin_signed)
    return outs[0] if len(outs) == 1 else jnp.concatenate(outs, axis=1)


def _qkprep_kernel(q_ref, kv_ref, cos_ref, sin_ref, gq_ref, gk_ref, bd_ref, qo_ref, ko_ref, vo_ref, *, rope):
    q = _head_rms(q_ref[0].astype(F32), bd_ref[...], gq_ref[...])
    kv = kv_ref[0]
    k = _head_rms(kv[:, :KV_WIDTH].astype(F32), bd_ref[:KV_WIDTH, :KV_WIDTH], gk_ref[...])
    if rope:
        q = _rope(q, cos_ref[...], sin_ref[...])
        k = _rope(k, cos_ref[...], sin_ref[...])
    q = q * (HEAD_DIM ** -0.5)
    for h in range(ATTN_HEADS):
        qo_ref[0, h] = q[:, h * HEAD_DIM:(h + 1) * HEAD_DIM].astype(BF16)
    v = kv[:, KV_WIDTH:].astype(F32)
    lane = lax.broadcasted_iota(I32, v.shape, 1)
    ones_col = jnp.where(lane == HEAD_DIM, 1.0, 0.0)
    for g in range(KV_HEADS):
        ko_ref[0, g] = k[:, g * HEAD_DIM:(g + 1) * HEAD_DIM].astype(BF16)
        vg = v if g == 0 else pltpu.roll(v, LANES - g * HEAD_DIM, 1)
        vo_ref[0, g] = jnp.where(lane < HEAD_DIM, vg, ones_col).astype(BF16)


def _qkprep_call(main, cos, sin, gq, gk, bd, tm, rope):
    b, t, _ = main.shape
    return pl.pallas_call(
        functools.partial(_qkprep_kernel, rope=rope),
        grid=(b, t // tm),
        in_specs=[pl.BlockSpec((1, tm, ATTN_WIDTH), lambda bi, i: (bi, i, COL_Q // ATTN_WIDTH)),
                  pl.BlockSpec((1, tm, 2 * KV_WIDTH), lambda bi, i: (bi, i, COL_KV // (2 * KV_WIDTH))),
                  pl.BlockSpec((tm, LANES), lambda bi, i: (i, 0)),
                  pl.BlockSpec((tm, LANES), lambda bi, i: (i, 0)),
                  pl.BlockSpec((1, ATTN_WIDTH), lambda bi, i: (0, 0)),
                  pl.BlockSpec((1, KV_WIDTH), lambda bi, i: (0, 0)),
                  pl.BlockSpec((ATTN_WIDTH, ATTN_WIDTH), lambda bi, i: (0, 0))],
        out_specs=[pl.BlockSpec((1, ATTN_HEADS, tm, HEAD_DIM), lambda bi, i: (bi, 0, i, 0)),
                   pl.BlockSpec((1, KV_HEADS, tm, HEAD_DIM), lambda bi, i: (bi, 0, i, 0)),
                   pl.BlockSpec((1, KV_HEADS, tm, LANES), lambda bi, i: (bi, 0, i, 0))],
        out_shape=[jax.ShapeDtypeStruct((b, ATTN_HEADS, t, HEAD_DIM), BF16),
                   jax.ShapeDtypeStruct((b, KV_HEADS, t, HEAD_DIM), BF16),
                   jax.ShapeDtypeStruct((b, KV_HEADS, t, LANES), BF16)],
        compiler_params=_cparams("parallel", "parallel"),
        name="qkprep",
    )(main, main, cos, sin, gq, gk, bd)


ATTN_BOUND_LIMIT = 40.0


def _attn_kernel(shift_ref, q_ref, *refs, bounded):
    o_ref = refs[-1]
    sources = list(zip(refs[:-1:2], refs[1:-1:2]))
    outs = []
    for r in range(ATTN_REP):
        q = q_ref[0, r]
        scores = [lax.dot_general(q, k_ref[0, 0], _NT, preferred_element_type=F32) for k_ref, _ in sources]
        if bounded:
            shift = shift_ref[0]
        else:
            shift = functools.reduce(jnp.maximum, [jnp.max(s, axis=-1, keepdims=True) for s in scores])
        acc = sum(_dot(jnp.exp(s - shift).astype(BF16), v_ref[0, 0]) for s, (_, v_ref) in zip(scores, sources))
        outs.append(acc[:, :HEAD_DIM] / acc[:, HEAD_DIM:HEAD_DIM + 1])
    o_ref[0] = jnp.concatenate(outs, axis=1).astype(o_ref.dtype)


def _attn_call(qh, kv_sources, bound, tq):
    b, _, t, _ = qh.shape
    in_specs = [pl.BlockSpec(memory_space=pltpu.SMEM),
                pl.BlockSpec((1, ATTN_REP, tq, HEAD_DIM), lambda bi, g, i: (bi, g, i, 0))]
    args = [bound.reshape(1), qh]
    for kh, vh in kv_sources:
        tk = kh.shape[2]
        in_specs += [pl.BlockSpec((1, 1, tk, HEAD_DIM), lambda bi, g, i: (bi, g, 0, 0)),
                     pl.BlockSpec((1, 1, tk, LANES), lambda bi, g, i: (bi, g, 0, 0))]
        args += [kh, vh]

    def call(bounded):
        return pl.pallas_call(
            functools.partial(_attn_kernel, bounded=bounded),
            grid=(b, KV_HEADS, t // tq),
            in_specs=in_specs,
            out_specs=pl.BlockSpec((1, tq, ATTN_REP * HEAD_DIM), lambda bi, g, i: (bi, i, g)),
            out_shape=jax.ShapeDtypeStruct((b, t, ATTN_WIDTH), BF16),
            compiler_params=_cparams("parallel", "parallel", "parallel"),
            name="attn_bounded" if bounded else "attn_rowmax",
        )(*args)

    return lax.cond(bound <= ATTN_BOUND_LIMIT, lambda: call(True), lambda: call(False))


def _conv3(u, prev_row, next_row, w):
    tm, s = u.shape[0], 8
    rid = lax.broadcasted_iota(I32, (s, u.shape[1]), 0)
    up = pltpu.roll(u, 1, 0)
    up = jnp.concatenate([jnp.where(rid == 0, prev_row, up[:s]), up[s:]], axis=0)
    dn = pltpu.roll(u, tm - 1, 0)
    dn = jnp.concatenate([dn[:tm - s], jnp.where(rid == s - 1, next_row, dn[tm - s:])], axis=0)
    return up * w[0:1, :] + u * w[1:2, :] + dn * w[2:3, :]


def _conv_kernel(x_ref, xp_ref, xn_ref, c_ref, cp_ref, cn_ref, s_ref, sp_ref, sn_ref, b_ref,
                 w_ref, bias_ref, wsc_ref, xo_ref, yo_ref):
    i, n = pl.program_id(1), pl.num_programs(1)
    keep_p = jnp.where(i > 0, 1.0, 0.0)
    keep_n = jnp.where(i < n - 1, 1.0, 0.0)
    last = BF16_ROWS - 1
    f = lambda r: r.astype(F32)
    u = f(x_ref[0])
    y = _conv3(u, f(xp_ref[0, last:last + 1, :]) * keep_p, f(xn_ref[0, 0:1, :]) * keep_n, w_ref[...]) + bias_ref[...]
    xo_ref[0] = _silu(y).astype(xo_ref.dtype)
    cs = f(c_ref[0]) * f(s_ref[0])
    cs_p = f(cp_ref[0, last:last + 1, :]) * f(sp_ref[0, last:last + 1, :]) * keep_p
    cs_n = f(cn_ref[0, 0:1, :]) * f(sn_ref[0, 0:1, :]) * keep_n
    yo_ref[0] = (f(b_ref[0]) * _conv3(cs, cs_p, cs_n, wsc_ref[...])).astype(yo_ref.dtype)


def _conv_call(main, conv_w, conv_b, sc_w, tm):
    b, t, _ = main.shape
    hb = tm // BF16_ROWS
    nh = t // BF16_ROWS

    def centre(width, col):
        return pl.BlockSpec((1, tm, width), lambda bi, i: (bi, i, col // width))

    def prev(width, col):
        return pl.BlockSpec((1, BF16_ROWS, width), lambda bi, i: (bi, jnp.maximum(i * hb - 1, 0), col // width))

    def nxt(width, col):
        return pl.BlockSpec((1, BF16_ROWS, width), lambda bi, i: (bi, jnp.minimum((i + 1) * hb, nh - 1), col // width))

    const = lambda shape: pl.BlockSpec(shape, lambda bi, i: (0, 0))
    return pl.pallas_call(
        _conv_kernel,
        grid=(b, t // tm),
        in_specs=[centre(SSD_CONV_DIM, COL_XBC), prev(SSD_CONV_DIM, COL_XBC), nxt(SSD_CONV_DIM, COL_XBC),
                  centre(SC_WIDTH, COL_SCC), prev(SC_WIDTH, COL_SCC), nxt(SC_WIDTH, COL_SCC),
                  centre(SC_WIDTH, COL_SCX), prev(SC_WIDTH, COL_SCX), nxt(SC_WIDTH, COL_SCX),
                  centre(SC_WIDTH, COL_SCB),
                  const((3, SSD_CONV_DIM)), const((1, SSD_CONV_DIM)), const((3, SC_WIDTH))],
        out_specs=[pl.BlockSpec((1, tm, SSD_CONV_DIM), lambda bi, i: (bi, i, 0)),
                   pl.BlockSpec((1, tm, SC_WIDTH), lambda bi, i: (bi, i, 0))],
        out_shape=[jax.ShapeDtypeStruct((b, t, SSD_CONV_DIM), BF16),
                   jax.ShapeDtypeStruct((b, t, SC_WIDTH), BF16)],
        compiler_params=_cparams("parallel", "parallel"),
        name="conv",
    )(main, main, main, main, main, main, main, main, main, main, conv_w, conv_b, sc_w)


def _ssd_kernel(xa_ref, dt_ref, dtT_ref, z_ref, s0_ref, brow_ref, arow_ref, bcol_ref, acol_ref, dexp_ref, nw_ref,
                y_ref, so_ref, yf_ref, st_ref):
    L = SSD_CHUNK
    t = xa_ref.shape[1]
    nc = t // L
    ri = lax.broadcasted_iota(I32, (L, L), 0)
    ci = lax.broadcasted_iota(I32, (L, L), 1)
    lane512 = lax.broadcasted_iota(I32, (L, SSD_INNER), 1)
    row512 = lax.broadcasted_iota(I32, (L, SSD_INNER), 0)
    blockmask = jnp.where((row512 >> 6) == (lane512 >> 8), 1.0, 0.0)
    lane128 = lax.broadcasted_iota(I32, (L, LANES), 1)
    a_row = -jnp.exp(arow_ref[...])
    a_col = -jnp.exp(acol_ref[...])

    st_ref[...] = s0_ref[0]

    def chunk(c, d):
        r0 = pl.multiple_of(c * L, L)
        causal = (ri >= ci) if d == 0 else (ri <= ci)
        tri = jnp.where(causal, 1.0, 0.0).astype(BF16)
        triT = jnp.where((ri <= ci) if d == 0 else (ri >= ci), 1.0, 0.0).astype(BF16)
        expand = jnp.where(row512 == (lane512 >> 6) + d * SSD_HEADS, 1.0, 0.0).astype(BF16)
        xa = xa_ref[0, pl.ds(r0, L), :]
        x = xa[:, :SSD_INNER]
        b_pair = xa[:, SSD_INNER:SSD_INNER + SSD_GN]
        c_pair = xa[:, SSD_INNER + SSD_GN:]
        dt_all = _softplus(dt_ref[0, pl.ds(r0, L), :] + brow_ref[...])
        a1, a2, a3 = _split3(dt_all * a_row)
        acs = _dot(tri, a1) + _dot(tri, a2) + _dot(tri, a3)
        dtT = _softplus(dtT_ref[0, :, pl.ds(r0, L)] + bcol_ref[...])
        acsT = _dot_exact_lhs(dtT * a_col, triT)
        edge = acs[L - 1:L, :] if d == 0 else acs[0:1, :]
        w_all = jnp.exp(edge - acs) * dt_all
        e_exp = jnp.exp(_dot_exact_lhs(acs, expand))
        w_exp = _dot(w_all.astype(BF16), expand)
        state = st_ref[d]
        y_off = _dot(c_pair, state.astype(BF16)) * e_exp
        gmats = [lax.dot_general(jnp.where((lane128 >> 6) == g, c_pair, jnp.zeros_like(c_pair)), b_pair, _NT,
                                 preferred_element_type=F32) for g in range(SSD_GROUPS)]
        pairs = []
        for pr in range(SSD_HEADS // 2):
            x_pair = x[:, pr * LANES:(pr + 1) * LANES]
            acc = jnp.zeros((L, LANES), F32)
            for hh in range(2):
                h = 2 * pr + hh
                gmat = gmats[h // (SSD_HEADS // SSD_GROUPS)]
                k = d * SSD_HEADS + h
                seg = acs[:, k:k + 1] - acsT[k:k + 1, :]
                m = jnp.exp(jnp.where(causal, seg, -jnp.inf)) * gmat * dtT[k:k + 1, :]
                xh = jnp.where((lane128 >> 6) == hh, x_pair, jnp.zeros_like(x_pair))
                acc = acc + _dot(m.astype(BF16), xh)
            pairs.append(acc)
        y = y_off + jnp.concatenate(pairs, axis=1)
        e_edge = e_exp[L - 1:L, :] if d == 0 else e_exp[0:1, :]
        ds = lax.dot_general(b_pair, (x.astype(F32) * w_exp).astype(BF16), _TN, preferred_element_type=F32)
        st_ref[d] = (state * e_edge + ds) * blockmask
        return y, x, r0

    def fwd(c, carry):
        y, _, r0 = chunk(c, 0)
        yf_ref[pl.ds(r0, L), :] = y
        return carry

    unroll = 4 if nc % 4 == 0 else (2 if nc % 2 == 0 else 1)
    lax.fori_loop(0, nc, fwd, 0, unroll=unroll)

    def bwd(i, carry):
        c = nc - 1 - i
        yb, x, r0 = chunk(c, 1)
        y = yf_ref[pl.ds(r0, L), :] + yb + dexp_ref[...] * x.astype(F32)
        gated = y * _silu(z_ref[0, pl.ds(r0, L), :].astype(F32))
        ms = jnp.mean(gated * gated, axis=-1, keepdims=True)
        y_ref[0, pl.ds(r0, L), :] = (gated * lax.rsqrt(ms + EPS) * nw_ref[...]).astype(y_ref.dtype)
        return carry

    lax.fori_loop(0, nc, bwd, 0, unroll=unroll)
    so_ref[0] = st_ref[...]


def _ssd_call(xa, dt, dtT, main, s0, brow, arow, bcol, acol, dexp, nw):
    b, t, _ = xa.shape
    c2 = lambda shape: pl.BlockSpec(shape, lambda bi: (0, 0))
    return pl.pallas_call(
        _ssd_kernel,
        grid=(b,),
        in_specs=[pl.BlockSpec((1, t, SSD_CONV_DIM), lambda bi: (bi, 0, 0)),
                  pl.BlockSpec((1, t, LANES), lambda bi: (bi, 0, 0)),
                  pl.BlockSpec((1, 2 * SSD_HEADS, t), lambda bi: (bi, 0, 0)),
                  pl.BlockSpec((1, t, SSD_INNER), lambda bi: (bi, 0, COL_Z // SSD_INNER)),
                  pl.BlockSpec((1, 2, SSD_GN, SSD_INNER), lambda bi: (bi, 0, 0, 0)),
                  c2((1, LANES)), c2((1, LANES)), c2((2 * SSD_HEADS, LANES)), c2((2 * SSD_HEADS, LANES)),
                  c2((1, SSD_INNER)), c2((1, SSD_INNER))],
        out_specs=[pl.BlockSpec((1, t, SSD_INNER), lambda bi: (bi, 0, 0)),
                   pl.BlockSpec((1, 2, SSD_GN, SSD_INNER), lambda bi: (bi, 0, 0, 0))],
        out_shape=[jax.ShapeDtypeStruct((b, t, SSD_INNER), BF16),
                   jax.ShapeDtypeStruct((b, 2, SSD_GN, SSD_INNER), F32)],
        scratch_shapes=[pltpu.VMEM((t, SSD_INNER), F32), pltpu.VMEM((2, SSD_GN, SSD_INNER), F32)],
        compiler_params=_cparams("parallel"),
        name="ssd",
    )(xa, dt, dtT, main, s0, brow, arow, bcol, acol, dexp, nw)


def _merge_kernel(ya_ref, ys_ref, yc_ref, g_ref, x_ref, m2_ref, m3_ref, m4_ref, wa_ref, ws_ref, wc_ref, wo_ref,
                  lg_ref, lb_ref, wr_ref, x1_ref, h2_ref, aff_ref, *, alpha):
    d = x_ref.shape[2]
    gate = lambda j: jax.nn.sigmoid(g_ref[0, :, j * d:(j + 1) * d].astype(F32))
    m = (gate(0) * _dot(ya_ref[0], wa_ref[...]) + gate(1) * _dot(ys_ref[0], ws_ref[...])
         + gate(2) * _dot(yc_ref[0], wc_ref[...]))
    out = _dot(m.astype(BF16), wo_ref[...])
    x1 = _layer_norm(alpha * x_ref[0] + m2_ref[0] * out) * lg_ref[...] + lb_ref[...]
    x1_ref[0] = x1
    h2 = _layer_norm(x1) * (1.0 + m4_ref[0]) + m3_ref[0]
    h2_ref[0] = h2.astype(h2_ref.dtype)
    logits = _dot_hp_nt(wr_ref[...], h2)
    e = jnp.exp(logits - jnp.max(logits, axis=0, keepdims=True))
    aff_ref[0] = e / jnp.sum(e, axis=0, keepdims=True)


def _merge_call(ya, ys, yc, main, x, mods, mrow, wa, ws, wc, wo, lg, lb, wrT, alpha, tm):
    b, t, d = x.shape
    tok = lambda w: pl.BlockSpec((1, tm, w), lambda bi, i: (bi, i, 0))
    const = lambda shape: pl.BlockSpec(shape, lambda bi, i: (0, 0))
    return pl.pallas_call(
        functools.partial(_merge_kernel, alpha=alpha),
        grid=(b, t // tm),
        in_specs=[tok(ATTN_WIDTH), tok(SSD_INNER), tok(SC_WIDTH), tok(N_BRANCH * d), tok(d),
                  _mod_spec(d, mrow, 2), _mod_spec(d, mrow, 3), _mod_spec(d, mrow, 4),
                  const((ATTN_WIDTH, d)), const((SSD_INNER, d)), const((SC_WIDTH, d)), const((d, d)),
                  const((1, d)), const((1, d)), const((N_EXPERTS, d))],
        out_specs=[tok(d), tok(d), pl.BlockSpec((1, N_EXPERTS, tm), lambda bi, i: (bi, 0, i))],
        out_shape=[jax.ShapeDtypeStruct((b, t, d), F32), jax.ShapeDtypeStruct((b, t, d), BF16),
                   jax.ShapeDtypeStruct((b, N_EXPERTS, t), F32)],
        compiler_params=_cparams("parallel", "parallel"),
        name="merge",
    )(ya, ys, yc, main, x, mods, mods, mods, wa, ws, wc, wo, lg, lb, wrT)


def _route_kernel(aff_ref, pos_ref, offs_ref, posT_ref, wT_ref, *, cap):
    aff = aff_ref[0]
    ne, t = aff.shape
    capf = float(cap)

    def enough(cand):
        return jnp.sum(jnp.where(aff >= pltpu.bitcast(cand, F32), 1.0, 0.0), axis=1, keepdims=True) >= capf

    def step(i, lo):
        b0 = jnp.left_shift(jnp.int32(1), 28 - 2 * i)
        b1 = b0 + b0
        c01, c10, c11 = lo | b0, lo | b1, lo | b1 | b0
        return jnp.where(enough(c11), c11, jnp.where(enough(c10), c10, jnp.where(enough(c01), c01, lo)))

    lo = jnp.zeros((ne, 1), I32)
    top = lo | jnp.int32(1 << 30)
    lo = jnp.where(enough(top), top, lo)
    thr = pltpu.bitcast(lax.fori_loop(0, 15, step, lo), F32)
    need = capf - jnp.sum(jnp.where(aff > thr, 1.0, 0.0), axis=1, keepdims=True)
    ri = lax.broadcasted_iota(I32, (LANES, LANES), 0)
    ci = lax.broadcasted_iota(I32, (LANES, LANES), 1)
    upper = jnp.where(ri < ci, 1.0, 0.0).astype(BF16)
    lane = lax.broadcasted_iota(I32, (ne, LANES), 1)
    carry_gt = jnp.zeros((ne, 1), F32)
    carry_eq = jnp.zeros((ne, 1), F32)
    offs = jnp.zeros((ne, LANES), F32)
    nblk = t // LANES
    for j in range(nblk):
        sl = slice(j * LANES, (j + 1) * LANES)
        aj = aff[:, sl]
        gj = jnp.where(aj > thr, 1.0, 0.0)
        ej = jnp.where(aj == thr, 1.0, 0.0)
        pre = _dot(jnp.concatenate([gj, ej], axis=0).astype(BF16), upper)
        pre_gt = pre[:ne] + carry_gt
        pre_eq = pre[ne:] + carry_eq
        sel = gj + ej * jnp.where(pre_eq < need, 1.0, 0.0)
        slot = pre_gt + jnp.minimum(pre_eq, need)
        pos_j = jnp.where(sel > 0.5, slot, -1.0)
        w_j = jnp.where(sel > 0.5, aj, 0.0)
        pos_ref[0, :, sl] = pos_j
        fill = jnp.zeros((LANES - ne, LANES), F32)
        posT_ref[0, sl, :] = jnp.concatenate([pos_j, fill], axis=0).T[:, :ne]
        wT_ref[0, sl, :] = jnp.concatenate([w_j, fill], axis=0).T[:, :ne]
        offs = jnp.where(lane == j, carry_gt + jnp.minimum(carry_eq, need), offs)
        carry_gt = carry_gt + jnp.sum(gj, axis=1, keepdims=True)
        carry_eq = carry_eq + jnp.sum(ej, axis=1, keepdims=True)
    offs = jnp.where(lane >= nblk, capf, offs)
    offs_ref[0] = offs.astype(I32)


def _route_call(affT, cap):
    b, ne, t = affT.shape
    spec = pl.BlockSpec((1, ne, t), lambda bi: (bi, 0, 0))
    spec_t = pl.BlockSpec((1, t, ne), lambda bi: (bi, 0, 0))
    return pl.pallas_call(
        functools.partial(_route_kernel, cap=cap),
        grid=(b,),
        in_specs=[spec],
        out_specs=[spec, pl.BlockSpec((1, ne, LANES), lambda bi: (bi, 0, 0)), spec_t, spec_t],
        out_shape=[jax.ShapeDtypeStruct((b, ne, t), F32),
                   jax.ShapeDtypeStruct((b, ne, LANES), I32),
                   jax.ShapeDtypeStruct((b, t, ne), F32), jax.ShapeDtypeStruct((b, t, ne), F32)],
        compiler_params=_cparams("parallel"),
        name="route",
    )(affT)


EXPERT_GROUP = 4


def _slot_windows(offs_ref, bi, i, ne, per, cap, win):
    out = []
    for e in range(ne):
        lo = offs_ref[(bi * ne + e) * LANES + i * per]
        hi = offs_ref[(bi * ne + e) * LANES + (i + 1) * per]
        out.append((lo, hi, lo - lax.rem(lo, BF16_ROWS)))
    return out


def _window_start(wb, p, cap, win):
    return pl.multiple_of(jnp.minimum(wb + p * win, cap - win), BF16_ROWS)


def _gather_kernel(offs_ref, h_ref, pos_ref, *rest, win):
    xs_ref = rest[-1]
    bi, i = pl.program_id(0), pl.program_id(1)
    tm = h_ref.shape[1]
    ne, cap = xs_ref.shape[1], xs_ref.shape[2]
    windows = _slot_windows(offs_ref, bi, i, ne, tm // LANES, cap, win)
    h = h_ref[0]
    pos = pos_ref[0]
    row = lax.broadcasted_iota(I32, (win, tm), 0)

    @pl.when(i == 0)
    def _():
        xs_ref[...] = jnp.zeros_like(xs_ref)

    def onehot(e, start, first=None):
        slot = row + start
        if first is not None:
            slot = jnp.where(slot >= first, slot, -2)
        return jnp.where(pos[e:e + 1, :] == slot.astype(F32), 1.0, 0.0).astype(BF16)

    def add_window(e, start, rows):
        xs_ref[0, e, pl.ds(start, win), :] += rows.astype(xs_ref.dtype)

    for g0 in range(0, ne, EXPERT_GROUP):
        experts = range(g0, min(g0 + EXPERT_GROUP, ne))
        starts = [_window_start(windows[e][2], 0, cap, win) for e in experts]
        y = _dot(jnp.concatenate([onehot(e, s) for e, s in zip(experts, starts)], axis=0), h)
        for k, (e, s) in enumerate(zip(experts, starts)):
            add_window(e, s, y[k * win:(k + 1) * win])
    for p in range(1, pl.cdiv(cap, win)):
        for e in range(ne):
            lo, hi, wb = windows[e]

            @pl.when(hi > wb + p * win)
            def _():
                s = _window_start(wb, p, cap, win)
                add_window(e, s, _dot(onehot(e, s, first=wb + p * win), h))


def _gather_call(offs, h2, pos, cap, tm, slots_total, slot_base, buf=None):
    b, t, d = h2.shape
    ne = pos.shape[1]
    mean = tm * cap // t
    win = min(LANES, cap, pl.cdiv(mean + mean // 4 + BF16_ROWS, BF16_ROWS) * BF16_ROWS)
    in_specs = [pl.BlockSpec((1, tm, d), lambda bi, i, offs: (bi, i, 0)),
                pl.BlockSpec((1, ne, tm), lambda bi, i, offs: (bi, 0, i))]
    args = [offs, h2, pos]
    if buf is not None:
        in_specs.append(pl.BlockSpec(memory_space=pl.ANY))
        args.append(buf)
    return pl.pallas_call(
        functools.partial(_gather_kernel, win=win),
        grid_spec=pltpu.PrefetchScalarGridSpec(
            num_scalar_prefetch=1,
            grid=(b, t // tm),
            in_specs=in_specs,
            out_specs=pl.BlockSpec((1, ne, cap, d), lambda bi, i, offs: (bi, 0, slot_base // cap, 0))),
        out_shape=jax.ShapeDtypeStruct((b, ne, slots_total, d), BF16),
        input_output_aliases={} if buf is None else {3: 0},
        compiler_params=_cparams("parallel", "arbitrary"),
        name="gather",
    )(*args)


def _ffn_kernel(xs_ref, wg_ref, wu_ref, wd_ref, o_ref, acc_ref, *, group):
    f, nf = pl.program_id(1), pl.num_programs(1)
    b, _, slots, d = xs_ref.shape
    kc = 256

    def dot_w(a, w_ref):
        k = w_ref.shape[2]
        return sum(_dot(a[:, k0:k0 + kc], w_ref[0, 0, k0:k0 + kc, :].astype(BF16)) for k0 in range(0, k, kc))

    @pl.when(f == 0)
    def _():
        acc_ref[...] = jnp.zeros_like(acc_ref)

    for b0 in range(0, b, group):
        x = xs_ref[b0:b0 + group, 0].reshape(group * slots, d)
        hid = (_silu(dot_w(x, wg_ref)) * dot_w(x, wu_ref)).astype(BF16)
        acc_ref[b0:b0 + group] += dot_w(hid, wd_ref).reshape(group, slots, d)

    @pl.when(f == nf - 1)
    def _():
        o_ref[:, 0] = acc_ref[...].astype(o_ref.dtype)


def _ffn_call(xs, wg, wu, wd, layer, tf):
    b, ne, slots, d = xs.shape
    ff = wg.shape[3]
    tok = pl.BlockSpec((b, 1, slots, d), lambda e, f: (0, e, 0, 0))
    return pl.pallas_call(
        functools.partial(_ffn_kernel, group=1),
        grid=(ne, ff // tf),
        in_specs=[tok,
                  pl.BlockSpec((1, 1, d, tf), lambda e, f: (layer, e, 0, f)),
                  pl.BlockSpec((1, 1, d, tf), lambda e, f: (layer, e, 0, f)),
                  pl.BlockSpec((1, 1, tf, d), lambda e, f: (layer, e, f, 0))],
        out_specs=tok,
        out_shape=jax.ShapeDtypeStruct(xs.shape, BF16),
        scratch_shapes=[pltpu.VMEM((b, slots, d), F32)],
        compiler_params=_cparams("parallel", "arbitrary"),
        name="ffn",
    )(xs, wg, wu, wd)


def _combine_kernel(offs_ref, o_ref, pos_ref, w_ref, x_ref, m5_ref, lg_ref, lb_ref, y_ref, acc_ref, *, alpha, win):
    bi, i = pl.program_id(0), pl.program_id(1)
    tm = x_ref.shape[1]
    ne, cap = o_ref.shape[1], o_ref.shape[2]
    windows = _slot_windows(offs_ref, bi, i, ne, tm // LANES, cap, win)
    pos = pos_ref[0]
    w = w_ref[0]
    col = lax.broadcasted_iota(I32, (tm, win), 1)

    def onehot(e, start, first=None):
        slot = col + start
        if first is not None:
            slot = jnp.where(slot >= first, slot, -2)
        return jnp.where(pos[:, e:e + 1] == slot.astype(F32), w[:, e:e + 1], 0.0).astype(BF16)

    y = jnp.zeros((tm, o_ref.shape[3]), F32)
    for g0 in range(0, ne, EXPERT_GROUP):
        experts = range(g0, min(g0 + EXPERT_GROUP, ne))
        starts = [_window_start(windows[e][2], 0, cap, win) for e in experts]
        lhs = jnp.concatenate([onehot(e, s) for e, s in zip(experts, starts)], axis=1)
        rhs = jnp.concatenate([o_ref[0, e, pl.ds(s, win), :] for e, s in zip(experts, starts)], axis=0)
        y = y + _dot(lhs, rhs)
    acc_ref[...] = y
    for p in range(1, cap // win):
        for e in range(ne):
            lo, hi, wb = windows[e]

            @pl.when(hi > wb + p * win)
            def _():
                s = _window_start(wb, p, cap, win)
                acc_ref[...] += _dot(onehot(e, s, first=wb + p * win), o_ref[0, e, pl.ds(s, win), :])
    y_ref[0] = _layer_norm(alpha * x_ref[0] + m5_ref[0] * acc_ref[...]) * lg_ref[...] + lb_ref[...]


def _combine_call(offs, out, posT, wT, x1, mods, mrow, lg, lb, alpha, tm, cap, slot_base):
    b, t, d = x1.shape
    ne = out.shape[1]
    return pl.pallas_call(
        functools.partial(_combine_kernel, alpha=alpha, win=min(LANES, cap)),
        grid_spec=pltpu.PrefetchScalarGridSpec(
            num_scalar_prefetch=1,
            grid=(b, t // tm),
            in_specs=[pl.BlockSpec((1, ne, cap, d), lambda bi, i, offs: (bi, 0, slot_base // cap, 0)),
                      pl.BlockSpec((1, tm, ne), lambda bi, i, offs: (bi, i, 0)),
                      pl.BlockSpec((1, tm, ne), lambda bi, i, offs: (bi, i, 0)),
                      pl.BlockSpec((1, tm, d), lambda bi, i, offs: (bi, i, 0)),
                      _mod_spec(d, mrow, 5, prefetch=True),
                      pl.BlockSpec((1, d), lambda bi, i, offs: (0, 0)),
                      pl.BlockSpec((1, d), lambda bi, i, offs: (0, 0))],
            out_specs=pl.BlockSpec((1, tm, d), lambda bi, i, offs: (bi, i, 0)),
            scratch_shapes=[pltpu.VMEM((tm, d), F32)]),
        out_shape=jax.ShapeDtypeStruct((b, t, d), F32),
        compiler_params=_cparams("parallel", "arbitrary"),
        name="combine",
    )(offs, out, posT, wT, x1, mods, lg, lb)


def _rope_tables(n):
    rows = n // GRID_W
    row = jnp.repeat(jnp.arange(rows), GRID_W).astype(F32)
    col = jnp.tile(jnp.arange(GRID_W), rows).astype(F32)
    inv = ROPE_THETA ** (-jnp.arange(0, AXIS_ROT, 2, dtype=F32) / AXIS_ROT)
    ang = jnp.stack([row[:, None] * inv, col[:, None] * inv], axis=1)
    cos = jnp.repeat(jnp.cos(ang)[:, :, None, :], 2, axis=2).reshape(n, HEAD_DIM)
    sin = jnp.stack([-jnp.sin(ang), jnp.sin(ang)], axis=2).reshape(n, HEAD_DIM)
    return jnp.tile(cos, (1, LANES // HEAD_DIM)), jnp.tile(sin, (1, LANES // HEAD_DIM))


def _tile(t, pref):
    return pref if t % pref == 0 else t


def _moe(streams, mods, wg, wu, wd, layer, lg, lb, alpha):
    caps = [CAPACITY_FACTOR * h2.shape[1] // N_EXPERTS for _, h2, _, _ in streams]
    bases = [sum(caps[:k]) for k in range(len(caps))]
    assert all(base % cap == 0 for base, cap in zip(bases, caps)) and sum(caps) % BF16_ROWS == 0
    xs, routed = None, []
    for (_, h2, affT, _), cap, base in zip(streams, caps, bases):
        t = h2.shape[1]
        pos, offs, posT, wT = _route_call(affT, cap)
        offs = offs.reshape(-1)
        xs = _gather_call(offs, h2, pos, cap, tm=_tile(t, 512), slots_total=sum(caps), slot_base=base, buf=xs)
        routed.append((offs, posT, wT))
    out = _ffn_call(xs, wg, wu, wd, layer, tf=_tile(wg.shape[3], 512))
    return [_combine_call(offs, out, posT, wT, x1, mods, mrow, lg, lb, alpha, tm=_tile(x1.shape[1], 512),
                          cap=cap, slot_base=base)
            for (x1, _, _, mrow), (offs, posT, wT), cap, base in zip(streams, routed, caps, bases)]


def kernel(x, c, ctx, c_ctx, w_mod, b_mod, w_in, q_norm, k_norm, ssd_conv_w, ssd_conv_b, ssd_a_log, ssd_dt_bias,
           ssd_d, ssd_norm, sc_conv_w, w_br_attn, w_br_ssd, w_br_conv, w_o, ln1_g, ln1_b, w_router, w_exp_gate,
           w_exp_up, w_exp_down, ln2_g, ln2_b):
    b, n, d = x.shape
    nctx = ctx.shape[1]
    depth = w_mod.shape[0]
    alpha = (2 * depth) ** 0.25

    rows = -(-(b + 1) // 8) * 8
    cvec = jnp.concatenate([c, c_ctx[None], jnp.zeros((rows - b - 1, d), F32)], axis=0)
    mods = _mod_call(cvec, w_mod, b_mod).reshape(depth * rows * N_MOD, 1, d)

    cos_l, sin_l = _rope_tables(n)
    cos_c, sin_c = jnp.ones((nctx, LANES), F32), jnp.zeros((nctx, LANES), F32)
    hid = jnp.arange(ATTN_WIDTH) // HEAD_DIM
    bd = (hid[:, None] == hid[None, :]).astype(BF16)
    dexp_all = jnp.repeat(ssd_d, SSD_HEAD_DIM, axis=1)
    pad16 = lambda v: jnp.pad(v.reshape(1, 2 * SSD_HEADS), ((0, 0), (0, LANES - 2 * SSD_HEADS)))
    col16 = lambda v: jnp.broadcast_to(v.reshape(2 * SSD_HEADS, 1), (2 * SSD_HEADS, LANES))

    w_in_t = jnp.swapaxes(w_in, 1, 2)
    w_main = w_in_t.astype(BF16)
    dt_lo, dt_hi = _REF_COLS["dt"]
    w_dt = jnp.pad(w_in_t[:, dt_lo:dt_hi, :], ((0, 0), (0, LANES - (dt_hi - dt_lo)), (0, 0)))
    x_ctx = ctx
    for i in range(depth):
        need_ctx = i < depth - 1
        gq = jnp.tile(q_norm[i], ATTN_HEADS)[None]
        gk = jnp.tile(k_norm[i], KV_HEADS)[None]
        brow, arow = pad16(ssd_dt_bias[i]), pad16(ssd_a_log[i])
        bcol, acol = col16(ssd_dt_bias[i]), col16(ssd_a_log[i])
        dexp, nw = dexp_all[i][None], ssd_norm[i][None]
        wa, ws, wc, wo = (w.astype(BF16) for w in (w_br_attn[i], w_br_ssd[i], w_br_conv[i], w_o[i]))
        wrT = w_router[i].T
        lg1, lb1, lg2, lb2 = ln1_g[i][None], ln1_b[i][None], ln2_g[i][None], ln2_b[i][None]
        mod_l = lambda bi, i=i: (i * rows + bi) * N_MOD
        mod_c = lambda bi, i=i: (i * rows + b) * N_MOD

        def mixer(xin, mod, cos, sin, rope, s0, groups=_MAIN_ORDER):
            t = xin.shape[1]
            main, dt, dtT = _inproj_call(xin, mods, mod, w_main, i, w_dt, tm=_tile(t, 512), groups=groups)
            qh, kh, vh = _qkprep_call(main, cos, sin, gq, gk, bd, tm=_tile(t, 512), rope=rope)
            xa, yc = _conv_call(main, ssd_conv_w[i], ssd_conv_b[i][None], sc_conv_w[i], tm=_tile(t, 512))
            ys, s_out = _ssd_call(xa, dt, dtT, main, s0, brow, arow, bcol, acol, dexp, nw)
            return main, qh, kh, vh, yc, ys, s_out

        def merge(xin, mod, main, ya, ys, yc):
            return _merge_call(ya, ys, yc, main, xin, mods, mod, wa, ws, wc, wo, lg1, lb1, wrT,
                               alpha, tm=_tile(xin.shape[1], 512))

        zero_state = jnp.zeros((b, 2, SSD_GN, SSD_INNER), F32)
        ctx_groups = _MAIN_ORDER if need_ctx else ("k", "v", "xbc")
        main_c, qh_c, kh_c, vh_c, yc_c, ys_c, s_ctx = mixer(x_ctx, mod_c, cos_c, sin_c, False, zero_state, ctx_groups)
        main_l, qh_l, kh_l, vh_l, yc_l, ys_l, _ = mixer(x, mod_l, cos_l, sin_l, True, s_ctx)
        bound = math.sqrt(HEAD_DIM) * jnp.max(jnp.abs(q_norm[i])) * jnp.max(jnp.abs(k_norm[i]))
        ya_l = _attn_call(qh_l, [(kh_c, vh_c), (kh_l, vh_l)], bound, tq=_tile(n, 256))
        streams = [(*merge(x, mod_l, main_l, ya_l, ys_l, yc_l), mod_l)]
        if need_ctx:
            ya_c = _attn_call(qh_c, [(kh_c, vh_c)], bound, tq=_tile(nctx, 256))
            streams.append((*merge(x_ctx, mod_c, main_c, ya_c, ys_c, yc_c), mod_c))
        new = _moe(streams, mods, w_exp_gate, w_exp_up, w_exp_down, i, lg2, lb2, alpha)
        x = new[0]
        if need_ctx:
            x_ctx = new[1]
    return x
```
